```python
import math
import jax, jax.numpy as jnp
from jax import lax
import numpy as np

D_MODEL = 1024
BATCH = 8
SEQ = 2048
DEPTH = 4
DEC_BATCH = 128
DEC_SEQ = 1
PAST_LEN = 16384
PAGE_SIZE = 128

GM_WIDTH = D_MODEL
GM_GROUPS = 4
GM_GROUP_DIM = GM_WIDTH // GM_GROUPS
GM_CHUNK = 128
RET_HEADS = 4
RET_DK = D_MODEL // RET_HEADS
RET_DV = 2 * RET_DK
RET_QK = RET_HEADS * RET_DK
RET_V = RET_HEADS * RET_DV
RET_CHUNK = 128
ROPE_BASE = 10000.0
D_FF = 4 * D_MODEL
EPS = 1e-6
IN_WIDTH = 2 * GM_WIDTH + 2 * RET_QK + 2 * RET_V + 2 * D_MODEL
SPLITS = [GM_WIDTH, 2 * GM_WIDTH, 2 * GM_WIDTH + RET_QK, 2 * GM_WIDTH + 2 * RET_QK,
          2 * GM_WIDTH + 2 * RET_QK + RET_V, 2 * GM_WIDTH + 2 * RET_QK + 2 * RET_V]

kernel_name = "hybrid_gmlp_retention_decoder_step"


def rms_norm(x, g):
    xf = x.astype(jnp.float32)
    y = xf * lax.rsqrt(jnp.mean(xf * xf, axis=-1, keepdims=True) + EPS)
    return (y * g.astype(jnp.float32)).astype(x.dtype)


def layer_norm(x, g, b):
    xf = x.astype(jnp.float32)
    mu = jnp.mean(xf, axis=-1, keepdims=True)
    xc = xf - mu
    y = xc * lax.rsqrt(jnp.mean(xc * xc, axis=-1, keepdims=True) + EPS)
    return (y * g.astype(jnp.float32) + b.astype(jnp.float32)).astype(x.dtype)


def chunk_spatial_gating(u, v, w_s, b_s):
    B, L, _ = v.shape
    n_chunks = -(-L // GM_CHUNK)
    pad = n_chunks * GM_CHUNK - L
    vp = jnp.pad(v, ((0, 0), (0, pad), (0, 0))).reshape(B, n_chunks, GM_CHUNK, GM_GROUPS, GM_GROUP_DIM)
    causal = jnp.tril(jnp.ones((GM_CHUNK, GM_CHUNK), dtype=bool))
    w = jnp.where(causal[None], w_s, jnp.zeros_like(w_s)).astype(v.dtype)
    mixed = jnp.einsum('gij,bcjgd->bcigd', w, vp) + b_s.T.astype(v.dtype)[None, None, :, :, None]
    mixed = mixed.reshape(B, n_chunks * GM_CHUNK, GM_WIDTH)[:, :L]
    return u * mixed


def rotary(x, pos0):
    L = x.shape[1]
    inv_freq = 1.0 / (ROPE_BASE ** jnp.linspace(0.0, 1.0, RET_DK // 2, dtype=jnp.float32))
    pos = (jnp.arange(L, dtype=jnp.int32) + pos0).astype(jnp.float32)
    ang = pos[:, None] * inv_freq[None, :]
    cos = jnp.cos(ang)[None, :, None, :]
    sin = jnp.sin(ang)[None, :, None, :]
    x1, x2 = jnp.split(x, 2, axis=-1)
    return jnp.concatenate([x1 * cos - x2 * sin, x1 * sin + x2 * cos], axis=-1)


def retention(q, k, v, s0):
    B, L = q.shape[:2]
    c = math.gcd(L, RET_CHUNK)
    n = L // c
    log_g = jnp.log(1.0 - 2.0 ** (-5.0 - jnp.arange(RET_HEADS, dtype=jnp.float32)))
    idx = jnp.arange(c, dtype=jnp.float32)
    diff = idx[:, None] - idx[None, :]
    intra = jnp.exp(jnp.where(diff[None] >= 0, diff[None] * log_g[:, None, None], -jnp.inf))
    q_decay = jnp.exp((idx[:, None] + 1.0) * log_g[None, :])
    k_decay = jnp.exp((c - 1.0 - idx[:, None]) * log_g[None, :])
    chunk_decay = jnp.exp(c * log_g)

    def to_chunks(t):
        return t.reshape(B, n, c, *t.shape[2:]).swapaxes(0, 1)

    def step(S, inp):
        qc, kc, vc = inp
        scores = jnp.einsum('bihd,bjhd->bhij', qc, kc) * intra[None]
        inner = jnp.einsum('bhij,bjhe->bihe', scores, vc)
        cross = jnp.einsum('bihd,bhde->bihe', qc, S) * q_decay[None, :, :, None]
        S_new = S * chunk_decay[None, :, None, None] + jnp.einsum(
            'bjhd,bjhe->bhde', kc * k_decay[None, :, :, None], vc)
        return S_new, inner + cross

    s_fin, o = lax.scan(step, s0, (to_chunks(q), to_chunks(k), to_chunks(v)))
    o = o.swapaxes(0, 1).reshape(B, L, RET_HEADS, RET_DV)
    return o, s_fin


def hybrid_mixer(xn, pos0, s0, w_in, b_gate, gm_ln_g, gm_ln_b, gm_w_s, gm_b_s,
                 w_proj_gm, w_proj_ret, w_out):
    B, L, _ = xn.shape
    proj = xn @ w_in
    u, v, q, k, vr, gr, gate = jnp.split(proj, SPLITS, axis=-1)
    u = jax.nn.gelu(u)
    v = layer_norm(jax.nn.gelu(v), gm_ln_g, gm_ln_b)
    a = chunk_spatial_gating(u, v, gm_w_s, gm_b_s)
    qf = rotary(q.reshape(B, L, RET_HEADS, RET_DK).astype(jnp.float32), pos0)
    kf = rotary(k.reshape(B, L, RET_HEADS, RET_DK).astype(jnp.float32), pos0) * (RET_DK ** -0.5)
    vf = vr.reshape(B, L, RET_HEADS, RET_DV).astype(jnp.float32)
    o, s_new = retention(qf, kf, vf, s0.astype(jnp.float32))
    o = o * lax.rsqrt(jnp.mean(o * o, axis=-1, keepdims=True) + EPS)
    r = jax.nn.silu(gr) * o.reshape(B, L, RET_V).astype(xn.dtype)
    gate_a, gate_b = jnp.split(jax.nn.sigmoid(gate + b_gate), 2, axis=-1)
    h = gate_a * (a @ w_proj_gm) + gate_b * (r @ w_proj_ret)
    return h @ w_out, s_new.astype(s0.dtype), v


def run_trunk(x, pos0, state_ret, norm_mix_g, w_in, b_gate, gm_ln_g, gm_ln_b, gm_w_s, gm_b_s,
              w_proj_gm, w_proj_ret, w_out, norm_ffn_g, w_up, w_down, norm_final_g):
    new_states = []
    v_rows = []
    for l in range(DEPTH):
        xn = rms_norm(x, norm_mix_g[l])
        mix, s_new, v = hybrid_mixer(xn, pos0, state_ret[l], w_in[l], b_gate[l], gm_ln_g[l], gm_ln_b[l],
                                     gm_w_s[l], gm_b_s[l], w_proj_gm[l], w_proj_ret[l], w_out[l])
        x = x + mix
        hn = rms_norm(x, norm_ffn_g[l])
        x = x + jnp.square(jax.nn.relu(hn @ w_up[l])) @ w_down[l]
        new_states.append(s_new)
        v_rows.append(v)
    return rms_norm(x, norm_final_g), jnp.stack(new_states), jnp.stack(v_rows)


def setup_inputs(seed: int = 0) -> dict:
    key = jax.random.key(seed)
    ks = jax.random.split(key, 20)
    f32 = jnp.float32

    def nrm(k, shape, scale):
        return jax.random.normal(k, shape, f32) * scale

    return {
        "x_prompt": nrm(ks[0], (BATCH, SEQ, D_MODEL), 1.0),
        "x_sample": nrm(ks[1], (DEC_BATCH, DEC_SEQ, D_MODEL), 1.0),
        "state_ret": nrm(ks[2], (DEPTH, DEC_BATCH, RET_HEADS, RET_DK, RET_DV), 0.5),
        "norm_mix_g": 1.0 + nrm(ks[3], (DEPTH, D_MODEL), 0.01),
        "w_in": nrm(ks[4], (DEPTH, D_MODEL, IN_WIDTH), D_MODEL ** -0.5),
        "b_gate": nrm(ks[5], (DEPTH, 2 * D_MODEL), 0.01),
        "gm_ln_g": 1.0 + nrm(ks[6], (DEPTH, GM_WIDTH), 0.01),
        "gm_ln_b": nrm(ks[7], (DEPTH, GM_WIDTH), 0.01),
        "gm_w_s": nrm(ks[8], (DEPTH, GM_GROUPS, GM_CHUNK, GM_CHUNK), GM_CHUNK ** -0.5),
        "gm_b_s": 1.0 + nrm(ks[9], (DEPTH, GM_GROUPS, GM_CHUNK), 0.01),
        "w_proj_gm": nrm(ks[10], (DEPTH, GM_WIDTH, D_MODEL), GM_WIDTH ** -0.5),
        "w_proj_ret": nrm(ks[11], (DEPTH, RET_V, D_MODEL), RET_V ** -0.5),
        "w_out": nrm(ks[12], (DEPTH, D_MODEL, D_MODEL), D_MODEL ** -0.5),
        "norm_ffn_g": 1.0 + nrm(ks[13], (DEPTH, D_MODEL), 0.01),
        "w_up": nrm(ks[14], (DEPTH, D_MODEL, D_FF), D_MODEL ** -0.5),
        "w_down": nrm(ks[15], (DEPTH, D_FF, D_MODEL), D_FF ** -0.5),
        "norm_final_g": 1.0 + nrm(ks[16], (D_MODEL,), 0.01),
    }


def reference(x_prompt, x_sample, state_ret, norm_mix_g, w_in, b_gate, gm_ln_g, gm_ln_b, gm_w_s, gm_b_s,
              w_proj_gm, w_proj_ret, w_out, norm_ffn_g, w_up, w_down, norm_final_g):
    zero_state = jnp.zeros((DEPTH, x_prompt.shape[0], RET_HEADS, RET_DK, RET_DV), state_ret.dtype)
    y_prompt, new_ret_prompt, _ = run_trunk(
        x_prompt, 0, zero_state, norm_mix_g, w_in, b_gate, gm_ln_g, gm_ln_b, gm_w_s, gm_b_s,
        w_proj_gm, w_proj_ret, w_out, norm_ffn_g, w_up, w_down, norm_final_g)
    y_sample, new_ret_sample, gm_v_sample = run_trunk(
        x_sample, PAST_LEN, state_ret, norm_mix_g, w_in, b_gate, gm_ln_g, gm_ln_b, gm_w_s, gm_b_s,
        w_proj_gm, w_proj_ret, w_out, norm_ffn_g, w_up, w_down, norm_final_g)
    return (y_prompt, y_sample, new_ret_prompt, new_ret_sample, gm_v_sample)
```

```python
import functools
import math

import numpy as np
import jax
import jax.numpy as jnp
from jax import lax
from jax.experimental import pallas as pl
from jax.experimental.pallas import tpu as pltpu

D_MODEL = 1024
DEPTH = 4
PAST_LEN = 16384
GM_WIDTH = D_MODEL
GM_GROUPS = 4
GM_GROUP_DIM = GM_WIDTH // GM_GROUPS
GM_CHUNK = 128
RET_HEADS = 4
RET_DK = D_MODEL // RET_HEADS
RET_DV = 2 * RET_DK
RET_QK = RET_HEADS * RET_DK
RET_V = RET_HEADS * RET_DV
ROPE_BASE = 10000.0
D_FF = 4 * D_MODEL
EPS = 1e-6

OFF_U = 0
OFF_V = GM_WIDTH
OFF_Q = 2 * GM_WIDTH
OFF_K = OFF_Q + RET_QK
OFF_VR = OFF_K + RET_QK
OFF_GR = OFF_VR + RET_V
OFF_GA = OFF_GR + RET_V
OFF_GB = OFF_GA + D_MODEL
IN_WIDTH = OFF_GB + D_MODEL

V7X_VMEM_BYTES = 64 * 1024 * 1024
SUBLANES = 8
LANES = 128

MIX_TILE = 256
FFN_TILE = 512
FF_SPLIT = 1024
STATE_ROWS = SUBLANES

BF16 = jnp.bfloat16
F32 = jnp.float32


def _dot(a, b):
    return jnp.dot(a, b, preferred_element_type=F32)


def _dot_nt(a, b):
    return lax.dot_general(a, b, (((1,), (1,)), ((), ())), preferred_element_type=F32)


def _dot_tn(a, b):
    return lax.dot_general(a, b, (((0,), (0,)), ((), ())), preferred_element_type=F32)


def _rms(x, g):
    return x * lax.rsqrt(jnp.mean(x * x, axis=-1, keepdims=True) + EPS) * g


def _layer_norm(x, g, b):
    mu = jnp.mean(x, axis=-1, keepdims=True)
    xc = x - mu
    return xc * lax.rsqrt(jnp.mean(xc * xc, axis=-1, keepdims=True) + EPS) * g + b


def _rotate(x, cos, sin):
    half = RET_DK // 2
    x1 = x[:, :half]
    x2 = x[:, half:]
    return jnp.concatenate([x1 * cos - x2 * sin, x1 * sin + x2 * cos], axis=-1)


def _head_gammas():
    return [1.0 - 2.0 ** (-5.0 - h) for h in range(RET_HEADS)]


def _mixer_kernel(x_ref, cos_ref, sin_ref, ng_ref, win_ref, bg_ref, lng_ref, lnb_ref, ws_ref,
                  bst_ref, intra_ref, qdec_ref, kdec_ref, wpg_ref, wpr_ref, wo_ref,
                  y_ref, s_ref, a_scr, r_scr, *, chunk_decay):
    t = x_ref.shape[0]

    @pl.when(pl.program_id(1) == 0)
    def _():
        s_ref[...] = jnp.zeros(s_ref.shape, F32)

    x = x_ref[...]
    xn = _rms(x, ng_ref[...]).astype(BF16)

    u = jax.nn.gelu(_dot(xn, win_ref[:, OFF_U:OFF_U + GM_WIDTH]))
    v = jax.nn.gelu(_dot(xn, win_ref[:, OFF_V:OFF_V + GM_WIDTH]))
    vb = _layer_norm(v, lng_ref[...], lnb_ref[...]).astype(BF16)
    row = lax.broadcasted_iota(jnp.int32, (GM_CHUNK, GM_CHUNK), 0)
    col = lax.broadcasted_iota(jnp.int32, (GM_CHUNK, GM_CHUNK), 1)
    for g in range(GM_GROUPS):
        w_tril = jnp.where(row >= col, ws_ref[g], 0.0).astype(BF16)
        b_col = bst_ref[:, g:g + 1]
        cs = slice(g * GM_GROUP_DIM, (g + 1) * GM_GROUP_DIM)
        for c in range(t // GM_CHUNK):
            rs = slice(c * GM_CHUNK, (c + 1) * GM_CHUNK)
            mixed = _dot(w_tril, vb[rs, cs]) + b_col
            a_scr[rs, cs] = (u[rs, cs] * mixed).astype(BF16)

    cos = cos_ref[...]
    sin = sin_ref[...]
    for h in range(RET_HEADS):
        q = _dot(xn, win_ref[:, OFF_Q + h * RET_DK:OFF_Q + (h + 1) * RET_DK])
        k = _dot(xn, win_ref[:, OFF_K + h * RET_DK:OFF_K + (h + 1) * RET_DK])
        vr = _dot(xn, win_ref[:, OFF_VR + h * RET_DV:OFF_VR + (h + 1) * RET_DV]).astype(BF16)
        gr = _dot(xn, win_ref[:, OFF_GR + h * RET_DV:OFF_GR + (h + 1) * RET_DV])
        qr = _rotate(q, cos, sin).astype(BF16)
        kr = _rotate(k, cos, sin) * (RET_DK ** -0.5)
        s_old = s_ref[h]
        scores = _dot_nt(qr, kr.astype(BF16)) * intra_ref[h]
        inner = _dot(scores.astype(BF16), vr)
        cross = _dot(qr, s_old.astype(BF16)) * qdec_ref[h]
        o = inner + cross
        s_ref[h] = s_old * chunk_decay[h] + _dot_tn((kr * kdec_ref[h]).astype(BF16), vr)
        on = o * lax.rsqrt(jnp.mean(o * o, axis=-1, keepdims=True) + EPS)
        r_scr[:, h * RET_DV:(h + 1) * RET_DV] = (jax.nn.silu(gr) * on).astype(BF16)

    ga = jax.nn.sigmoid(_dot(xn, win_ref[:, OFF_GA:OFF_GA + D_MODEL]) + bg_ref[:, :D_MODEL])
    hm = ga * _dot(a_scr[...], wpg_ref[...])
    gb = jax.nn.sigmoid(_dot(xn, win_ref[:, OFF_GB:OFF_GB + D_MODEL]) + bg_ref[:, D_MODEL:])
    hm = hm + gb * _dot(r_scr[...], wpr_ref[...])
    y_ref[...] = x + _dot(hm.astype(BF16), wo_ref[...])


def _retention_constants(c):
    log_g = np.log(np.array(_head_gammas(), np.float64))
    idx = np.arange(c, dtype=np.float64)
    diff = idx[:, None] - idx[None, :]
    intra = np.where(diff[None] >= 0, np.exp(diff[None] * log_g[:, None, None]), 0.0)
    qdec = np.exp((idx[None, :, None] + 1.0) * log_g[:, None, None])
    kdec = np.exp((c - 1.0 - idx[None, :, None]) * log_g[:, None, None])
    chunk_decay = tuple(float(v) for v in np.exp(c * log_g).astype(np.float32))
    return (jnp.asarray(intra, F32), jnp.asarray(qdec, F32), jnp.asarray(kdec, F32), chunk_decay)


def _rope_tables(pos0, length):
    inv_freq = 1.0 / (ROPE_BASE ** jnp.linspace(0.0, 1.0, RET_DK // 2, dtype=F32))
    pos = (jnp.arange(length, dtype=jnp.int32) + pos0).astype(F32)
    ang = pos[:, None] * inv_freq[None, :]
    return jnp.cos(ang), jnp.sin(ang)


def _full(shape):
    n = len(shape)
    return pl.BlockSpec(shape, lambda *_: (0,) * n)


def _mixer_call(x, cos, sin, ng, win, bg, lng, lnb, ws, bst, intra, qdec, kdec, wpg, wpr, wo,
                chunk_decay):
    b, l, d = x.shape
    t = MIX_TILE
    grid = (b, l // t)
    tok = pl.BlockSpec((None, t, d), lambda i, j: (i, j, 0))
    rope = pl.BlockSpec((t, RET_DK // 2), lambda i, j: (j, 0))
    in_specs = [tok, rope, rope, _full(ng.shape), _full(win.shape), _full(bg.shape),
                _full(lng.shape), _full(lnb.shape), _full(ws.shape), _full(bst.shape),
                _full(intra.shape), _full(qdec.shape), _full(kdec.shape), _full(wpg.shape),
                _full(wpr.shape), _full(wo.shape)]
    out_specs = [tok, pl.BlockSpec((None, RET_HEADS, RET_DK, RET_DV), lambda i, j: (i, 0, 0, 0))]
    out_shape = [jax.ShapeDtypeStruct((b, l, d), F32),
                 jax.ShapeDtypeStruct((b, RET_HEADS, RET_DK, RET_DV), F32)]
    return pl.pallas_call(
        functools.partial(_mixer_kernel, chunk_decay=chunk_decay),
        grid=grid, in_specs=in_specs, out_specs=out_specs, out_shape=out_shape,
        scratch_shapes=[pltpu.VMEM((t, GM_WIDTH), BF16), pltpu.VMEM((t, RET_V), BF16)],
        compiler_params=pltpu.CompilerParams(
            dimension_semantics=("arbitrary", "arbitrary"),
            vmem_limit_bytes=V7X_VMEM_BYTES - 8 * 1024 * 1024),
        name="prompt_mixer",
    )(x, cos, sin, ng, win, bg, lng, lnb, ws, bst, intra, qdec, kdec, wpg, wpr, wo)


def _ffn_body(x, g, wup_ref, wdn_ref):
    hn = _rms(x, g).astype(BF16)
    acc = x
    for j in range(D_FF // FF_SPLIT):
        up = _dot(hn, wup_ref[:, j * FF_SPLIT:(j + 1) * FF_SPLIT])
        act = jnp.square(jnp.maximum(up, 0.0)).astype(BF16)
        acc = acc + _dot(act, wdn_ref[j * FF_SPLIT:(j + 1) * FF_SPLIT, :])
    return acc


def _ffn_kernel(x_ref, g_ref, wup_ref, wdn_ref, gf_ref, y_ref, *, final):
    y = _ffn_body(x_ref[...], g_ref[...], wup_ref, wdn_ref)
    if final:
        y = _rms(y, gf_ref[...])
    y_ref[...] = y


def _ffn_call(x2d, g, wup, wdn, gf, final):
    n, d = x2d.shape
    t = FFN_TILE
    tok = pl.BlockSpec((t, d), lambda i: (i, 0))
    return pl.pallas_call(
        functools.partial(_ffn_kernel, final=final),
        grid=(n // t,),
        in_specs=[tok, _full(g.shape), _full(wup.shape), _full(wdn.shape), _full(gf.shape)],
        out_specs=tok,
        out_shape=jax.ShapeDtypeStruct((n, d), F32),
        compiler_params=pltpu.CompilerParams(
            dimension_semantics=("arbitrary",),
            vmem_limit_bytes=V7X_VMEM_BYTES - 16 * 1024 * 1024),
        name="prompt_ffn",
    )(x2d, g, wup, wdn, gf)


def _sample_pre_kernel(x_ref, cos_ref, sin_ref, ng_ref, win_ref, bg_ref, lng_ref, lnb_ref,
                       wsd_ref, bsd_ref, v_ref, a_ref, q_ref, k_ref, vr_ref, sg_ref, gate_ref):
    xn = _rms(x_ref[...], ng_ref[...]).astype(BF16)
    u = jax.nn.gelu(_dot(xn, win_ref[:, OFF_U:OFF_U + GM_WIDTH]))
    v = jax.nn.gelu(_dot(xn, win_ref[:, OFF_V:OFF_V + GM_WIDTH]))
    v = _layer_norm(v, lng_ref[...], lnb_ref[...])
    v_ref[...] = v
    a_ref[...] = (u * (wsd_ref[...] * v + bsd_ref[...])).astype(BF16)
    cos = cos_ref[...]
    sin = sin_ref[...]
    for h in range(RET_HEADS):
        qs = slice(h * RET_DK, (h + 1) * RET_DK)
        q = _dot(xn, win_ref[:, OFF_Q + h * RET_DK:OFF_Q + (h + 1) * RET_DK])
        k = _dot(xn, win_ref[:, OFF_K + h * RET_DK:OFF_K + (h + 1) * RET_DK])
        q_ref[:, qs] = _rotate(q, cos, sin)
        k_ref[:, qs] = _rotate(k, cos, sin) * (RET_DK ** -0.5)
    vr_ref[...] = _dot(xn, win_ref[:, OFF_VR:OFF_VR + RET_V])
    sg_ref[...] = jax.nn.silu(_dot(xn, win_ref[:, OFF_GR:OFF_GR + RET_V]))
    gate_ref[...] = jax.nn.sigmoid(_dot(xn, win_ref[:, OFF_GA:OFF_GA + 2 * D_MODEL]) + bg_ref[...])


def _sample_pre_call(x, cos, sin, ng, win, bg, lng, lnb, wsd, bsd):
    n = x.shape[0]
    out_shape = [jax.ShapeDtypeStruct((n, GM_WIDTH), F32),
                 jax.ShapeDtypeStruct((n, GM_WIDTH), BF16),
                 jax.ShapeDtypeStruct((n, RET_QK), F32),
                 jax.ShapeDtypeStruct((n, RET_QK), F32),
                 jax.ShapeDtypeStruct((n, RET_V), F32),
                 jax.ShapeDtypeStruct((n, RET_V), F32),
                 jax.ShapeDtypeStruct((n, 2 * D_MODEL), F32)]
    return pl.pallas_call(
        _sample_pre_kernel, out_shape=out_shape,
        compiler_params=pltpu.CompilerParams(vmem_limit_bytes=V7X_VMEM_BYTES - 16 * 1024 * 1024),
        name="sample_pre",
    )(x, cos, sin, ng, win, bg, lng, lnb, wsd, bsd)


def _sample_state_kernel(*refs, gammas, aliased):
    if aliased:
        s_ref, qt_ref, kt_ref, v_ref, _, snew_ref, o_ref = refs
    else:
        s_ref, qt_ref, kt_ref, v_ref, snew_ref, o_ref = refs
    h = pl.program_id(1)
    gamma = jnp.where(h == 0, gammas[0],
                      jnp.where(h == 1, gammas[1], jnp.where(h == 2, gammas[2], gammas[3])))
    gamma = gamma.astype(F32)
    for j in range(STATE_ROWS):
        k_col = kt_ref[:, j:j + 1]
        q_col = qt_ref[:, j:j + 1]
        v_row = v_ref[j:j + 1, :]
        s_new = s_ref[j] * gamma + k_col * v_row
        snew_ref[j] = s_new
        o_ref[j:j + 1, :] = jnp.sum(q_col * s_new, axis=0, keepdims=True)


def _sample_state_call(layer, state_ret, qt, kt, v, prev_out):
    depth, n = state_ret.shape[:2]
    grid = (n // STATE_ROWS, RET_HEADS)
    s_spec = pl.BlockSpec((None, STATE_ROWS, None, RET_DK, RET_DV),
                          lambda i, h: (layer, i, h, 0, 0))
    col_spec = pl.BlockSpec((None, RET_DK, STATE_ROWS), lambda i, h: (i, h, 0))
    v_spec = pl.BlockSpec((STATE_ROWS, RET_DV), lambda i, h: (i, h))
    in_specs = [s_spec, col_spec, col_spec, v_spec]
    args = [state_ret, qt, kt, v]
    aliases = {}
    if prev_out is not None:
        in_specs.append(pl.BlockSpec(memory_space=pl.ANY))
        args.append(prev_out)
        aliases = {4: 0}
    return pl.pallas_call(
        functools.partial(_sample_state_kernel, gammas=tuple(_head_gammas()),
                          aliased=prev_out is not None),
        grid=grid, in_specs=in_specs,
        out_specs=[s_spec, v_spec],
        out_shape=[jax.ShapeDtypeStruct(state_ret.shape, F32),
                   jax.ShapeDtypeStruct((n, RET_V), F32)],
        input_output_aliases=aliases,
        compiler_params=pltpu.CompilerParams(dimension_semantics=("arbitrary", "arbitrary")),
        name="sample_state",
    )(*args)


def _sample_post_kernel(x_ref, o_ref, sg_ref, gate_ref, a_ref, wpg_ref, wpr_ref, wo_ref,
                        nfg_ref, wup_ref, wdn_ref, gf_ref, y_ref, *, final):
    rs = []
    for h in range(RET_HEADS):
        hs = slice(h * RET_DV, (h + 1) * RET_DV)
        o = o_ref[:, hs]
        on = o * lax.rsqrt(jnp.mean(o * o, axis=-1, keepdims=True) + EPS)
        rs.append((sg_ref[:, hs] * on).astype(BF16))
    r = jnp.concatenate(rs, axis=-1)
    hm = (gate_ref[:, :D_MODEL] * _dot(a_ref[...], wpg_ref[...])
          + gate_ref[:, D_MODEL:] * _dot(r, wpr_ref[...]))
    x = x_ref[...] + _dot(hm.astype(BF16), wo_ref[...])
    y = _ffn_body(x, nfg_ref[...], wup_ref, wdn_ref)
    if final:
        y = _rms(y, gf_ref[...])
    y_ref[...] = y


def _sample_post_call(x, o, sg, gate, a, wpg, wpr, wo, nfg, wup, wdn, gf, final):
    return pl.pallas_call(
        functools.partial(_sample_post_kernel, final=final),
        out_shape=jax.ShapeDtypeStruct(x.shape, F32),
        compiler_params=pltpu.CompilerParams(vmem_limit_bytes=V7X_VMEM_BYTES - 16 * 1024 * 1024),
        name="sample_post",
    )(x, o, sg, gate, a, wpg, wpr, wo, nfg, wup, wdn, gf)


def _columns(m):
    n = m.shape[0]
    return m.reshape(n // STATE_ROWS, STATE_ROWS, RET_QK).transpose(0, 2, 1)


def kernel(x_prompt, x_sample, state_ret, norm_mix_g, w_in, b_gate, gm_ln_g, gm_ln_b, gm_w_s, gm_b_s,
           w_proj_gm, w_proj_ret, w_out, norm_ffn_g, w_up, w_down, norm_final_g):
    b, l, d = x_prompt.shape
    n = x_sample.shape[0]
    assert l % MIX_TILE == 0 and (b * l) % FFN_TILE == 0 and n % STATE_ROWS == 0
    assert x_sample.shape[1] == 1

    intra, qdec, kdec, chunk_decay = _retention_constants(MIX_TILE)
    cos_p, sin_p = _rope_tables(0, l)
    cos_s, sin_s = _rope_tables(PAST_LEN, 1)
    row = lambda a: a.reshape(1, -1)
    gf = row(norm_final_g)

    xp = x_prompt
    xs = x_sample.reshape(n, d)
    prompt_states = []
    v_rows = []
    new_state = None
    for layer in range(DEPTH):
        win = w_in[layer].astype(BF16)
        wpg = w_proj_gm[layer].astype(BF16)
        wpr = w_proj_ret[layer].astype(BF16)
        wo = w_out[layer].astype(BF16)
        wup = w_up[layer].astype(BF16)
        wdn = w_down[layer].astype(BF16)
        ng = row(norm_mix_g[layer])
        bg = row(b_gate[layer])
        lng = row(gm_ln_g[layer])
        lnb = row(gm_ln_b[layer])
        nfg = row(norm_ffn_g[layer])
        final = layer == DEPTH - 1

        xp, s_p = _mixer_call(xp, cos_p, sin_p, ng, win, bg, lng, lnb, gm_w_s[layer],
                              gm_b_s[layer].T, intra, qdec, kdec, wpg, wpr, wo, chunk_decay)
        prompt_states.append(s_p)
        xp = _ffn_call(xp.reshape(b * l, d), nfg, wup, wdn, gf, final).reshape(b, l, d)

        wsd = row(jnp.repeat(gm_w_s[layer, :, 0, 0], GM_GROUP_DIM))
        bsd = row(jnp.repeat(gm_b_s[layer, :, 0], GM_GROUP_DIM))
        v_s, a_s, q_s, k_s, vr_s, sg_s, gate_s = _sample_pre_call(
            xs, cos_s, sin_s, ng, win, bg, lng, lnb, wsd, bsd)
        v_rows.append(v_s)
        new_state, o_s = _sample_state_call(layer, state_ret, _columns(q_s), _columns(k_s), vr_s,
                                            new_state)
        xs = _sample_post_call(xs, o_s, sg_s, gate_s, a_s, wpg, wpr, wo, nfg, wup, wdn, gf, final)

    return (xp, xs.reshape(n, 1, d), jnp.stack(prompt_states), new_state,
            jnp.stack(v_rows).reshape(DEPTH, n, 1, GM_WIDTH))
```

```python
import functools

import numpy as np
import jax
import jax.numpy as jnp
from jax import lax
from jax.experimental import pallas as pl
from jax.experimental.pallas import tpu as pltpu

D_MODEL = 1024
DEPTH = 4
PAST_LEN = 16384
GM_WIDTH = D_MODEL
GM_GROUPS = 4
GM_GROUP_DIM = GM_WIDTH // GM_GROUPS
GM_CHUNK = 128
RET_HEADS = 4
RET_DK = D_MODEL // RET_HEADS
RET_DV = 2 * RET_DK
RET_QK = RET_HEADS * RET_DK
RET_V = RET_HEADS * RET_DV
ROPE_BASE = 10000.0
D_FF = 4 * D_MODEL
EPS = 1e-6

OFF_U = 0
OFF_V = GM_WIDTH
OFF_Q = 2 * GM_WIDTH
OFF_K = OFF_Q + RET_QK
OFF_VR = OFF_K + RET_QK
OFF_GR = OFF_VR + RET_V
OFF_GA = OFF_GR + RET_V
OFF_GB = OFF_GA + D_MODEL
IN_WIDTH = OFF_GB + D_MODEL

V7X_VMEM_BYTES = 64 * 1024 * 1024
MIB = 1024 * 1024

MIX_TILE = 256
FFN_TILE = 256
FF_SPLIT = 1024

BF16 = jnp.bfloat16
F32 = jnp.float32


def _dot(a, b):
    return jnp.dot(a, b, preferred_element_type=F32)


def _dot_nt(a, b):
    return lax.dot_general(a, b, (((1,), (1,)), ((), ())), preferred_element_type=F32)


def _dot_tn(a, b):
    return lax.dot_general(a, b, (((0,), (0,)), ((), ())), preferred_element_type=F32)


def _rms(x, g):
    return x * lax.rsqrt(jnp.mean(x * x, axis=-1, keepdims=True) + EPS) * g


def _layer_norm(x, g, b):
    mu = jnp.mean(x, axis=-1, keepdims=True)
    xc = x - mu
    return xc * lax.rsqrt(jnp.mean(xc * xc, axis=-1, keepdims=True) + EPS) * g + b


def _rotate(x, cos, sin):
    half = RET_DK // 2
    x1 = x[:, :half]
    x2 = x[:, half:]
    return jnp.concatenate([x1 * cos - x2 * sin, x1 * sin + x2 * cos], axis=-1)


def _head_gammas():
    return tuple(1.0 - 2.0 ** (-5.0 - h) for h in range(RET_HEADS))


def _full(shape):
    n = len(shape)
    return pl.BlockSpec(shape, lambda *_: (0,) * n)


def _layer_spec(stacked, layer):
    tail = stacked.shape[1:]
    zeros = (0,) * len(tail)
    return pl.BlockSpec((None,) + tail, lambda *_: (layer,) + zeros, pipeline_mode=pl.Buffered(1))


def _mixer_kernel(*refs, chunk_decay):
    (x_ref, cos_ref, sin_ref, ng_ref, win_ref, bg_ref, lng_ref, lnb_ref, ws_ref, bst_ref,
     intra_ref, qdec_ref, kdec_ref, wpg_ref, wpr_ref, wo_ref) = refs[:16]
    y_ref, s_ref, a_scr, r_scr = refs[-4:]
    t = x_ref.shape[0]

    @pl.when(pl.program_id(1) == 0)
    def _():
        s_ref[...] = jnp.zeros(s_ref.shape, F32)

    x = x_ref[...]
    xn = _rms(x, ng_ref[...]).astype(BF16)

    u = jax.nn.gelu(_dot(xn, win_ref[:, OFF_U:OFF_U + GM_WIDTH]))
    v = jax.nn.gelu(_dot(xn, win_ref[:, OFF_V:OFF_V + GM_WIDTH]))
    vb = _layer_norm(v, lng_ref[...], lnb_ref[...]).astype(BF16)
    row = lax.broadcasted_iota(jnp.int32, (GM_CHUNK, GM_CHUNK), 0)
    col = lax.broadcasted_iota(jnp.int32, (GM_CHUNK, GM_CHUNK), 1)
    for g in range(GM_GROUPS):
        w_tril = jnp.where(row >= col, ws_ref[g], 0.0).astype(BF16)
        b_col = bst_ref[:, g:g + 1]
        cs = slice(g * GM_GROUP_DIM, (g + 1) * GM_GROUP_DIM)
        for c in range(t // GM_CHUNK):
            rs = slice(c * GM_CHUNK, (c + 1) * GM_CHUNK)
            mixed = _dot(w_tril, vb[rs, cs]) + b_col
            a_scr[rs, cs] = (u[rs, cs] * mixed).astype(BF16)

    cos = cos_ref[...]
    sin = sin_ref[...]
    for h in range(RET_HEADS):
        q = _dot(xn, win_ref[:, OFF_Q + h * RET_DK:OFF_Q + (h + 1) * RET_DK])
        k = _dot(xn, win_ref[:, OFF_K + h * RET_DK:OFF_K + (h + 1) * RET_DK])
        vr = _dot(xn, win_ref[:, OFF_VR + h * RET_DV:OFF_VR + (h + 1) * RET_DV]).astype(BF16)
        gr = _dot(xn, win_ref[:, OFF_GR + h * RET_DV:OFF_GR + (h + 1) * RET_DV])
        qr = _rotate(q, cos, sin).astype(BF16)
        kr = _rotate(k, cos, sin) * (RET_DK ** -0.5)
        s_old = s_ref[h]
        scores = _dot_nt(qr, kr.astype(BF16)) * intra_ref[h]
        inner = _dot(scores.astype(BF16), vr)
        cross = _dot(qr, s_old.astype(BF16)) * qdec_ref[h]
        o = inner + cross
        s_ref[h] = s_old * chunk_decay[h] + _dot_tn((kr * kdec_ref[h]).astype(BF16), vr)
        on = o * lax.rsqrt(jnp.mean(o * o, axis=-1, keepdims=True) + EPS)
        r_scr[:, h * RET_DV:(h + 1) * RET_DV] = (jax.nn.silu(gr) * on).astype(BF16)

    ga = jax.nn.sigmoid(_dot(xn, win_ref[:, OFF_GA:OFF_GA + D_MODEL]) + bg_ref[:, :D_MODEL])
    hm = ga * _dot(a_scr[...], wpg_ref[...])
    gb = jax.nn.sigmoid(_dot(xn, win_ref[:, OFF_GB:OFF_GB + D_MODEL]) + bg_ref[:, D_MODEL:])
    hm = hm + gb * _dot(r_scr[...], wpr_ref[...])
    y_ref[...] = x + _dot(hm.astype(BF16), wo_ref[...])


def _retention_constants(c):
    log_g = np.log(np.array(_head_gammas(), np.float64))
    idx = np.arange(c, dtype=np.float64)
    diff = idx[:, None] - idx[None, :]
    intra = np.where(diff[None] >= 0, np.exp(diff[None] * log_g[:, None, None]), 0.0)
    qdec = np.exp((idx[None, :, None] + 1.0) * log_g[:, None, None])
    kdec = np.exp((c - 1.0 - idx[None, :, None]) * log_g[:, None, None])
    chunk_decay = tuple(float(v) for v in np.exp(c * log_g).astype(np.float32))
    return (jnp.asarray(intra, F32), jnp.asarray(qdec, F32), jnp.asarray(kdec, F32), chunk_decay)


def _rope_tables(pos0, length):
    inv_freq = 1.0 / (ROPE_BASE ** jnp.linspace(0.0, 1.0, RET_DK // 2, dtype=F32))
    pos = (jnp.arange(length, dtype=jnp.int32) + pos0).astype(F32)
    ang = pos[:, None] * inv_freq[None, :]
    return jnp.cos(ang), jnp.sin(ang)


def _mixer_call(layer, x, cos, sin, ng, win, bg, lng, lnb, ws, bst, intra, qdec, kdec, wpg, wpr,
                wo, chunk_decay, prev_states):
    b, l, d = x.shape
    t = MIX_TILE
    tok = pl.BlockSpec((None, t, d), lambda i, j: (i, j, 0))
    rope = pl.BlockSpec((t, RET_DK // 2), lambda i, j: (j, 0))
    in_specs = [tok, rope, rope, _full(ng.shape), _layer_spec(win, layer), _full(bg.shape),
                _full(lng.shape), _full(lnb.shape), _full(ws.shape), _full(bst.shape),
                _full(intra.shape), _full(qdec.shape), _full(kdec.shape),
                _layer_spec(wpg, layer), _layer_spec(wpr, layer), _layer_spec(wo, layer)]
    args = [x, cos, sin, ng, win, bg, lng, lnb, ws, bst, intra, qdec, kdec, wpg, wpr, wo]
    aliases = {}
    if prev_states is not None:
        in_specs.append(pl.BlockSpec(memory_space=pl.ANY))
        args.append(prev_states)
        aliases = {len(args) - 1: 1}
    state_spec = pl.BlockSpec((None, None, RET_HEADS, RET_DK, RET_DV),
                              lambda i, j: (layer, i, 0, 0, 0))
    return pl.pallas_call(
        functools.partial(_mixer_kernel, chunk_decay=chunk_decay),
        grid=(b, l // t), in_specs=in_specs, out_specs=[tok, state_spec],
        out_shape=[jax.ShapeDtypeStruct((b, l, d), F32),
                   jax.ShapeDtypeStruct((DEPTH, b, RET_HEADS, RET_DK, RET_DV), F32)],
        scratch_shapes=[pltpu.VMEM((t, GM_WIDTH), BF16), pltpu.VMEM((t, RET_V), BF16)],
        input_output_aliases=aliases,
        compiler_params=pltpu.CompilerParams(
            dimension_semantics=("arbitrary", "arbitrary"),
            vmem_limit_bytes=V7X_VMEM_BYTES - 8 * MIB),
        name="prompt_mixer",
    )(*args)


def _ffn_body(x, g, wup_ref, wdn_ref):
    hn = _rms(x, g).astype(BF16)
    acc = x
    for j in range(D_FF // FF_SPLIT):
        up = _dot(hn, wup_ref[:, j * FF_SPLIT:(j + 1) * FF_SPLIT])
        act = jnp.square(jnp.maximum(up, 0.0)).astype(BF16)
        acc = acc + _dot(act, wdn_ref[j * FF_SPLIT:(j + 1) * FF_SPLIT, :])
    return acc


def _ffn_state_kernel(*refs, final, gammas, seqs):
    x_ref, g_ref, wup_ref, wdn_ref, gf_ref, s_ref, qk_ref, v_ref = refs[:8]
    y_ref, snew_ref, o_ref = refs[-3:]

    for j in range(seqs):
        for h in range(RET_HEADS):
            q_col = qk_ref[:, j * RET_HEADS + h:j * RET_HEADS + h + 1]
            k_col = qk_ref[:, (seqs + j) * RET_HEADS + h:(seqs + j) * RET_HEADS + h + 1]
            v_row = v_ref[j:j + 1, h * RET_DV:(h + 1) * RET_DV]
            s_new = s_ref[j, h] * gammas[h] + k_col * v_row
            snew_ref[j, h] = s_new
            o_ref[j:j + 1, h * RET_DV:(h + 1) * RET_DV] = jnp.sum(q_col * s_new, axis=0,
                                                                  keepdims=True)

    y = _ffn_body(x_ref[...], g_ref[...], wup_ref, wdn_ref)
    if final:
        y = _rms(y, gf_ref[...])
    y_ref[...] = y


def _ffn_state_call(layer, x2d, g, wup, wdn, gf, final, state_ret, q_s, k_s, v_s, prev_out):
    tokens, d = x2d.shape
    n = q_s.shape[0]
    t = FFN_TILE
    steps = tokens // t
    seqs = n // steps
    assert steps * t == tokens and seqs * steps == n

    q4 = q_s.reshape(steps, seqs, RET_HEADS, RET_DK)
    k4 = k_s.reshape(steps, seqs, RET_HEADS, RET_DK)
    qk = jnp.stack([q4, k4], axis=0).transpose(1, 4, 0, 2, 3).reshape(
        steps, RET_DK, 2 * seqs * RET_HEADS)
    v3 = v_s.reshape(steps, seqs, RET_V)

    tok = pl.BlockSpec((t, d), lambda i: (i, 0))
    s_spec = pl.BlockSpec((None, seqs, RET_HEADS, RET_DK, RET_DV), lambda i: (layer, i, 0, 0, 0))
    qk_spec = pl.BlockSpec((None, RET_DK, 2 * seqs * RET_HEADS), lambda i: (i, 0, 0))
    v_spec = pl.BlockSpec((None, seqs, RET_V), lambda i: (i, 0, 0))
    in_specs = [tok, _full(g.shape), _layer_spec(wup, layer), _layer_spec(wdn, layer),
                _full(gf.shape), s_spec, qk_spec, v_spec]
    args = [x2d, g, wup, wdn, gf, state_ret, qk, v3]
    aliases = {}
    if prev_out is not None:
        in_specs.append(pl.BlockSpec(memory_space=pl.ANY))
        args.append(prev_out)
        aliases = {len(args) - 1: 1}
    y, new_state, o3 = pl.pallas_call(
        functools.partial(_ffn_state_kernel, final=final, gammas=_head_gammas(), seqs=seqs),
        grid=(steps,), in_specs=in_specs, out_specs=[tok, s_spec, v_spec],
        out_shape=[jax.ShapeDtypeStruct((tokens, d), F32),
                   jax.ShapeDtypeStruct(state_ret.shape, F32),
                   jax.ShapeDtypeStruct((steps, seqs, RET_V), F32)],
        input_output_aliases=aliases,
        compiler_params=pltpu.CompilerParams(
            dimension_semantics=("arbitrary",),
            vmem_limit_bytes=V7X_VMEM_BYTES - 8 * MIB),
        name="prompt_ffn_sample_state",
    )(*args)
    return y, new_state, o3.reshape(n, RET_V)


def _sample_pre_kernel(x_ref, cos_ref, sin_ref, ng_ref, win_ref, bg_ref, lng_ref, lnb_ref,
                       wsd_ref, bsd_ref, v_ref, a_ref, q_ref, k_ref, vr_ref, sg_ref, gate_ref):
    xn = _rms(x_ref[...], ng_ref[...]).astype(BF16)
    u = jax.nn.gelu(_dot(xn, win_ref[:, OFF_U:OFF_U + GM_WIDTH]))
    v = jax.nn.gelu(_dot(xn, win_ref[:, OFF_V:OFF_V + GM_WIDTH]))
    v = _layer_norm(v, lng_ref[...], lnb_ref[...])
    v_ref[...] = v
    a_ref[...] = (u * (wsd_ref[...] * v + bsd_ref[...])).astype(BF16)
    cos = cos_ref[...]
    sin = sin_ref[...]
    for h in range(RET_HEADS):
        qs = slice(h * RET_DK, (h + 1) * RET_DK)
        q = _dot(xn, win_ref[:, OFF_Q + h * RET_DK:OFF_Q + (h + 1) * RET_DK])
        k = _dot(xn, win_ref[:, OFF_K + h * RET_DK:OFF_K + (h + 1) * RET_DK])
        q_ref[:, qs] = _rotate(q, cos, sin)
        k_ref[:, qs] = _rotate(k, cos, sin) * (RET_DK ** -0.5)
    vr_ref[...] = _dot(xn, win_ref[:, OFF_VR:OFF_VR + RET_V])
    sg_ref[...] = jax.nn.silu(_dot(xn, win_ref[:, OFF_GR:OFF_GR + RET_V]))
    gate_ref[...] = jax.nn.sigmoid(_dot(xn, win_ref[:, OFF_GA:OFF_GA + 2 * D_MODEL]) + bg_ref[...])


def _sample_pre_call(layer, x, cos, sin, ng, win, bg, lng, lnb, wsd, bsd):
    n = x.shape[0]
    out_shape = [jax.ShapeDtypeStruct((n, GM_WIDTH), F32),
                 jax.ShapeDtypeStruct((n, GM_WIDTH), BF16),
                 jax.ShapeDtypeStruct((n, RET_QK), F32),
                 jax.ShapeDtypeStruct((n, RET_QK), F32),
                 jax.ShapeDtypeStruct((n, RET_V), F32),
                 jax.ShapeDtypeStruct((n, RET_V), F32),
                 jax.ShapeDtypeStruct((n, 2 * D_MODEL), F32)]
    args = [x, cos, sin, ng, win, bg, lng, lnb, wsd, bsd]
    in_specs = [_layer_spec(a, layer) if a is win else _full(a.shape) for a in args]
    return pl.pallas_call(
        _sample_pre_kernel, grid=(1,), in_specs=in_specs,
        out_specs=[_full(s.shape) for s in out_shape], out_shape=out_shape,
        compiler_params=pltpu.CompilerParams(
            dimension_semantics=("arbitrary",), vmem_limit_bytes=V7X_VMEM_BYTES - 16 * MIB),
        name="sample_pre",
    )(*args)


def _sample_post_kernel(x_ref, o_ref, sg_ref, gate_ref, a_ref, wpg_ref, wpr_ref, wo_ref,
                        nfg_ref, wup_ref, wdn_ref, gf_ref, y_ref, *, final):
    rs = []
    for h in range(RET_HEADS):
        hs = slice(h * RET_DV, (h + 1) * RET_DV)
        o = o_ref[:, hs]
        on = o * lax.rsqrt(jnp.mean(o * o, axis=-1, keepdims=True) + EPS)
        rs.append((sg_ref[:, hs] * on).astype(BF16))
    r = jnp.concatenate(rs, axis=-1)
    hm = (gate_ref[:, :D_MODEL] * _dot(a_ref[...], wpg_ref[...])
          + gate_ref[:, D_MODEL:] * _dot(r, wpr_ref[...]))
    x = x_ref[...] + _dot(hm.astype(BF16), wo_ref[...])
    y = _ffn_body(x, nfg_ref[...], wup_ref, wdn_ref)
    if final:
        y = _rms(y, gf_ref[...])
    y_ref[...] = y


def _sample_post_call(layer, x, o, sg, gate, a, wpg, wpr, wo, nfg, wup, wdn, gf, final):
    args = [x, o, sg, gate, a, wpg, wpr, wo, nfg, wup, wdn, gf]
    stacked = (wpg, wpr, wo, wup, wdn)
    in_specs = [_layer_spec(v, layer) if any(v is w for w in stacked) else _full(v.shape)
                for v in args]
    return pl.pallas_call(
        functools.partial(_sample_post_kernel, final=final),
        grid=(1,), in_specs=in_specs, out_specs=_full(x.shape),
        out_shape=jax.ShapeDtypeStruct(x.shape, F32),
        compiler_params=pltpu.CompilerParams(
            dimension_semantics=("arbitrary",), vmem_limit_bytes=V7X_VMEM_BYTES - 16 * MIB),
        name="sample_post",
    )(*args)


def kernel(x_prompt, x_sample, state_ret, norm_mix_g, w_in, b_gate, gm_ln_g, gm_ln_b, gm_w_s, gm_b_s,
           w_proj_gm, w_proj_ret, w_out, norm_ffn_g, w_up, w_down, norm_final_g):
    b, l, d = x_prompt.shape
    n = x_sample.shape[0]
    assert l % MIX_TILE == 0 and x_sample.shape[1] == 1

    intra, qdec, kdec, chunk_decay = _retention_constants(MIX_TILE)
    cos_p, sin_p = _rope_tables(0, l)
    cos_s, sin_s = _rope_tables(PAST_LEN, 1)
    row = lambda a: a.reshape(1, -1)
    gf = row(norm_final_g)
    win, wpg, wpr, wo, wup, wdn = (w.astype(BF16) for w in
                                   (w_in, w_proj_gm, w_proj_ret, w_out, w_up, w_down))

    xp = x_prompt
    xs = x_sample.reshape(n, d)
    v_rows = []
    prompt_states = None
    sample_states = None
    for layer in range(DEPTH):
        ng = row(norm_mix_g[layer])
        bg = row(b_gate[layer])
        lng = row(gm_ln_g[layer])
        lnb = row(gm_ln_b[layer])
        nfg = row(norm_ffn_g[layer])
        final = layer == DEPTH - 1

        wsd = row(jnp.repeat(gm_w_s[layer, :, 0, 0], GM_GROUP_DIM))
        bsd = row(jnp.repeat(gm_b_s[layer, :, 0], GM_GROUP_DIM))
        v_s, a_s, q_s, k_s, vr_s, sg_s, gate_s = _sample_pre_call(
            layer, xs, cos_s, sin_s, ng, win, bg, lng, lnb, wsd, bsd)
        v_rows.append(v_s)

        xp, prompt_states = _mixer_call(layer, xp, cos_p, sin_p, ng, win, bg, lng, lnb,
                                        gm_w_s[layer], gm_b_s[layer].T, intra, qdec, kdec,
                                        wpg, wpr, wo, chunk_decay, prompt_states)
        xp, sample_states, o_s = _ffn_state_call(layer, xp.reshape(b * l, d), nfg, wup, wdn, gf,
                                                 final, state_ret, q_s, k_s, vr_s, sample_states)
        xp = xp.reshape(b, l, d)
        xs = _sample_post_call(layer, xs, o_s, sg_s, gate_s, a_s, wpg, wpr, wo, nfg, wup, wdn,
                               gf, final)

    return (xp, xs.reshape(n, 1, d), prompt_states, sample_states,
            jnp.stack(v_rows).reshape(DEPTH, n, 1, GM_WIDTH))
```

```python
import functools

import numpy as np
import jax
import jax.numpy as jnp
from jax import lax
from jax.experimental import pallas as pl
from jax.experimental.pallas import tpu as pltpu

D_MODEL = 1024
DEPTH = 4
PAST_LEN = 16384
GM_WIDTH = D_MODEL
GM_GROUPS = 4
GM_GROUP_DIM = GM_WIDTH // GM_GROUPS
GM_CHUNK = 128
RET_HEADS = 4
RET_DK = D_MODEL // RET_HEADS
RET_DV = 2 * RET_DK
RET_QK = RET_HEADS * RET_DK
RET_V = RET_HEADS * RET_DV
ROPE_BASE = 10000.0
D_FF = 4 * D_MODEL
EPS = 1e-6

OFF_U = 0
OFF_V = GM_WIDTH
OFF_Q = 2 * GM_WIDTH
OFF_K = OFF_Q + RET_QK
OFF_VR = OFF_K + RET_QK
OFF_GR = OFF_VR + RET_V
OFF_GA = OFF_GR + RET_V
OFF_GB = OFF_GA + D_MODEL
IN_WIDTH = OFF_GB + D_MODEL

V7X_VMEM_BYTES = 64 * 1024 * 1024
MIB = 1024 * 1024

MIX_TILE = 512
RET_CHUNK = 256
FFN_TILE = 256
FF_SPLIT = 1024

BF16 = jnp.bfloat16
F32 = jnp.float32


def _dot(a, b):
    return jnp.dot(a, b, preferred_element_type=F32)


def _dot_nt(a, b):
    return lax.dot_general(a, b, (((1,), (1,)), ((), ())), preferred_element_type=F32)


def _dot_tn(a, b):
    return lax.dot_general(a, b, (((0,), (0,)), ((), ())), preferred_element_type=F32)


def _rms(x, g):
    return x * lax.rsqrt(jnp.mean(x * x, axis=-1, keepdims=True) + EPS) * g


def _layer_norm(x, g, b):
    mu = jnp.mean(x, axis=-1, keepdims=True)
    xc = x - mu
    return xc * lax.rsqrt(jnp.mean(xc * xc, axis=-1, keepdims=True) + EPS) * g + b


def _rotate(x, cos, sin):
    half = RET_DK // 2
    x1 = x[:, :half]
    x2 = x[:, half:]
    return jnp.concatenate([x1 * cos - x2 * sin, x1 * sin + x2 * cos], axis=-1)


def _head_gammas():
    return tuple(1.0 - 2.0 ** (-5.0 - h) for h in range(RET_HEADS))


def _full(shape):
    n = len(shape)
    return pl.BlockSpec(shape, lambda *_: (0,) * n)


def _cast_specs(stacked, layer, steps, step_of):
    _, rows, cols = stacked.shape
    r = rows // steps
    assert r * steps == rows
    src = pl.BlockSpec((None, r, cols), lambda *idx: (layer, step_of(*idx), 0))
    dst = pl.BlockSpec((r, cols), lambda *idx: (step_of(*idx), 0))
    return src, dst


def _cast_rows(cast_in, cast_out):
    for src, dst in zip(cast_in, cast_out):
        dst[...] = src[...].astype(BF16)


def _mixer_kernel(*refs, chunk_decay, n_cast):
    (x_ref, cos_ref, sin_ref, ng_ref, win_ref, bg_ref, lng_ref, lnb_ref, ws_ref, bst_ref,
     intra_ref, qdec_ref, kdec_ref, wpg_ref, wpr_ref, wo_ref) = refs[:16]
    cast_in = refs[16:16 + n_cast]
    y_ref, s_ref = refs[-4 - n_cast:-2 - n_cast]
    cast_out = refs[-2 - n_cast:-2]
    a_scr, r_scr = refs[-2:]
    c = RET_CHUNK

    _cast_rows(cast_in, cast_out)

    @pl.when(pl.program_id(1) == 0)
    def _():
        s_ref[...] = jnp.zeros(s_ref.shape, F32)

    row = lax.broadcasted_iota(jnp.int32, (GM_CHUNK, GM_CHUNK), 0)
    col = lax.broadcasted_iota(jnp.int32, (GM_CHUNK, GM_CHUNK), 1)

    def chunk_body(i, carry):
        rows = pl.ds(pl.multiple_of(i * c, c), c)
        x = x_ref[rows, :]
        xn = _rms(x, ng_ref[...]).astype(BF16)

        u = jax.nn.gelu(_dot(xn, win_ref[:, OFF_U:OFF_U + GM_WIDTH]))
        v = jax.nn.gelu(_dot(xn, win_ref[:, OFF_V:OFF_V + GM_WIDTH]))
        ga = jax.nn.sigmoid(_dot(xn, win_ref[:, OFF_GA:OFF_GA + D_MODEL]) + bg_ref[:, :D_MODEL])
        gb = jax.nn.sigmoid(_dot(xn, win_ref[:, OFF_GB:OFF_GB + D_MODEL]) + bg_ref[:, D_MODEL:])

        vb = _layer_norm(v, lng_ref[...], lnb_ref[...]).astype(BF16)
        for g in range(GM_GROUPS):
            w_tril = jnp.where(row >= col, ws_ref[g], 0.0).astype(BF16)
            b_col = bst_ref[:, g:g + 1]
            cs = slice(g * GM_GROUP_DIM, (g + 1) * GM_GROUP_DIM)
            for k in range(c // GM_CHUNK):
                rs = slice(k * GM_CHUNK, (k + 1) * GM_CHUNK)
                mixed = _dot(w_tril, vb[rs, cs]) + b_col
                a_scr[rs, cs] = (u[rs, cs] * mixed).astype(BF16)

        cos = cos_ref[rows, :]
        sin = sin_ref[rows, :]
        for h in range(RET_HEADS):
            q = _dot(xn, win_ref[:, OFF_Q + h * RET_DK:OFF_Q + (h + 1) * RET_DK])
            k = _dot(xn, win_ref[:, OFF_K + h * RET_DK:OFF_K + (h + 1) * RET_DK])
            vr = _dot(xn, win_ref[:, OFF_VR + h * RET_DV:OFF_VR + (h + 1) * RET_DV]).astype(BF16)
            gr = _dot(xn, win_ref[:, OFF_GR + h * RET_DV:OFF_GR + (h + 1) * RET_DV])
            qr = _rotate(q, cos, sin).astype(BF16)
            kr = _rotate(k, cos, sin) * (RET_DK ** -0.5)
            s_old = s_ref[h]
            scores = _dot_nt(qr, kr.astype(BF16)) * intra_ref[h]
            inner = _dot(scores.astype(BF16), vr)
            cross = _dot(qr, s_old.astype(BF16)) * qdec_ref[h]
            o = inner + cross
            s_ref[h] = s_old * chunk_decay[h] + _dot_tn((kr * kdec_ref[h]).astype(BF16), vr)
            on = o * lax.rsqrt(jnp.mean(o * o, axis=-1, keepdims=True) + EPS)
            r_scr[:, h * RET_DV:(h + 1) * RET_DV] = (jax.nn.silu(gr) * on).astype(BF16)

        hm = ga * _dot(a_scr[...], wpg_ref[...]) + gb * _dot(r_scr[...], wpr_ref[...])
        y_ref[rows, :] = x + _dot(hm.astype(BF16), wo_ref[...])
        return carry

    lax.fori_loop(0, x_ref.shape[0] // c, chunk_body, 0)


def _retention_constants(c):
    log_g = np.log(np.array(_head_gammas(), np.float64))
    idx = np.arange(c, dtype=np.float64)
    diff = idx[:, None] - idx[None, :]
    intra = np.where(diff[None] >= 0, np.exp(diff[None] * log_g[:, None, None]), 0.0)
    qdec = np.exp((idx[None, :, None] + 1.0) * log_g[:, None, None])
    kdec = np.exp((c - 1.0 - idx[None, :, None]) * log_g[:, None, None])
    chunk_decay = tuple(float(v) for v in np.exp(c * log_g).astype(np.float32))
    return (jnp.asarray(intra, F32), jnp.asarray(qdec, F32), jnp.asarray(kdec, F32), chunk_decay)


def _rope_tables(pos0, length):
    inv_freq = 1.0 / (ROPE_BASE ** jnp.linspace(0.0, 1.0, RET_DK // 2, dtype=F32))
    pos = (jnp.arange(length, dtype=jnp.int32) + pos0).astype(F32)
    ang = pos[:, None] * inv_freq[None, :]
    return jnp.cos(ang), jnp.sin(ang)


def _mixer_call(layer, x, cos, sin, ng, win, bg, lng, lnb, ws, bst, intra, qdec, kdec, wpg, wpr,
                wo, chunk_decay, prev_states, cast_srcs):
    b, l, d = x.shape
    t = MIX_TILE
    per_row = l // t
    steps = b * per_row
    tok = pl.BlockSpec((None, t, d), lambda i, j: (i, j, 0))
    rope = pl.BlockSpec((t, RET_DK // 2), lambda i, j: (j, 0))
    in_specs = [tok, rope, rope, _full(ng.shape), _full(win.shape), _full(bg.shape),
                _full(lng.shape), _full(lnb.shape), _full(ws.shape), _full(bst.shape),
                _full(intra.shape), _full(qdec.shape), _full(kdec.shape),
                _full(wpg.shape), _full(wpr.shape), _full(wo.shape)]
    args = [x, cos, sin, ng, win, bg, lng, lnb, ws, bst, intra, qdec, kdec, wpg, wpr, wo]
    cast_out_specs = []
    cast_out_shapes = []
    for w in cast_srcs:
        src, dst = _cast_specs(w, layer, steps, lambda i, j: i * per_row + j)
        in_specs.append(src)
        args.append(w)
        cast_out_specs.append(dst)
        cast_out_shapes.append(jax.ShapeDtypeStruct(w.shape[1:], BF16))
    aliases = {}
    if prev_states is not None:
        in_specs.append(pl.BlockSpec(memory_space=pl.ANY))
        args.append(prev_states)
        aliases = {len(args) - 1: 1}
    state_spec = pl.BlockSpec((None, None, RET_HEADS, RET_DK, RET_DV),
                              lambda i, j: (layer, i, 0, 0, 0))
    outs = pl.pallas_call(
        functools.partial(_mixer_kernel, chunk_decay=chunk_decay, n_cast=len(cast_srcs)),
        grid=(b, per_row), in_specs=in_specs,
        out_specs=[tok, state_spec] + cast_out_specs,
        out_shape=[jax.ShapeDtypeStruct((b, l, d), F32),
                   jax.ShapeDtypeStruct((DEPTH, b, RET_HEADS, RET_DK, RET_DV), F32)]
        + cast_out_shapes,
        scratch_shapes=[pltpu.VMEM((RET_CHUNK, GM_WIDTH), BF16),
                        pltpu.VMEM((RET_CHUNK, RET_V), BF16)],
        input_output_aliases=aliases,
        compiler_params=pltpu.CompilerParams(
            dimension_semantics=("arbitrary", "arbitrary"),
            vmem_limit_bytes=V7X_VMEM_BYTES - 6 * MIB),
        name="prompt_mixer",
    )(*args)
    return outs[0], outs[1], tuple(outs[2:])


def _ffn_body(x, g, wup_ref, wdn_ref):
    hn = _rms(x, g).astype(BF16)
    acc = x
    for j in range(D_FF // FF_SPLIT):
        up = _dot(hn, wup_ref[:, j * FF_SPLIT:(j + 1) * FF_SPLIT])
        act = jnp.square(jnp.maximum(up, 0.0)).astype(BF16)
        acc = acc + _dot(act, wdn_ref[j * FF_SPLIT:(j + 1) * FF_SPLIT, :])
    return acc


def _ffn_state_kernel(*refs, final, gammas, seqs, n_cast):
    x_ref, g_ref, wup_ref, wdn_ref, gf_ref, s_ref, qk_ref, v_ref = refs[:8]
    cast_in = refs[8:8 + n_cast]
    y_ref, snew_ref, o_ref = refs[-3 - n_cast:len(refs) - n_cast]
    cast_out = refs[len(refs) - n_cast:]

    _cast_rows(cast_in, cast_out)

    for j in range(seqs):
        for h in range(RET_HEADS):
            q_col = qk_ref[:, j * RET_HEADS + h:j * RET_HEADS + h + 1]
            k_col = qk_ref[:, (seqs + j) * RET_HEADS + h:(seqs + j) * RET_HEADS + h + 1]
            v_row = v_ref[j:j + 1, h * RET_DV:(h + 1) * RET_DV]
            s_new = s_ref[j, h] * gammas[h] + k_col * v_row
            snew_ref[j, h] = s_new
            o_ref[j:j + 1, h * RET_DV:(h + 1) * RET_DV] = jnp.sum(q_col * s_new, axis=0,
                                                                  keepdims=True)

    y = _ffn_body(x_ref[...], g_ref[...], wup_ref, wdn_ref)
    if final:
        y = _rms(y, gf_ref[...])
    y_ref[...] = y


def _ffn_state_call(layer, x2d, g, wup, wdn, gf, final, state_ret, q_s, k_s, v_s, prev_out,
                    cast_srcs):
    tokens, d = x2d.shape
    n = q_s.shape[0]
    t = FFN_TILE
    steps = tokens // t
    seqs = n // steps
    assert steps * t == tokens and seqs * steps == n

    q4 = q_s.reshape(steps, seqs, RET_HEADS, RET_DK)
    k4 = k_s.reshape(steps, seqs, RET_HEADS, RET_DK)
    qk = jnp.stack([q4, k4], axis=0).transpose(1, 4, 0, 2, 3).reshape(
        steps, RET_DK, 2 * seqs * RET_HEADS)
    v3 = v_s.reshape(steps, seqs, RET_V)

    tok = pl.BlockSpec((t, d), lambda i: (i, 0))
    s_spec = pl.BlockSpec((None, seqs, RET_HEADS, RET_DK, RET_DV), lambda i: (layer, i, 0, 0, 0))
    qk_spec = pl.BlockSpec((None, RET_DK, 2 * seqs * RET_HEADS), lambda i: (i, 0, 0))
    v_spec = pl.BlockSpec((None, seqs, RET_V), lambda i: (i, 0, 0))
    in_specs = [tok, _full(g.shape), _full(wup.shape), _full(wdn.shape), _full(gf.shape),
                s_spec, qk_spec, v_spec]
    args = [x2d, g, wup, wdn, gf, state_ret, qk, v3]
    cast_out_specs = []
    cast_out_shapes = []
    for w in cast_srcs:
        src, dst = _cast_specs(w, layer + 1, steps, lambda i: i)
        in_specs.append(src)
        args.append(w)
        cast_out_specs.append(dst)
        cast_out_shapes.append(jax.ShapeDtypeStruct(w.shape[1:], BF16))
    aliases = {}
    if prev_out is not None:
        in_specs.append(pl.BlockSpec(memory_space=pl.ANY))
        args.append(prev_out)
        aliases = {len(args) - 1: 1}
    outs = pl.pallas_call(
        functools.partial(_ffn_state_kernel, final=final, gammas=_head_gammas(), seqs=seqs,
                          n_cast=len(cast_srcs)),
        grid=(steps,), in_specs=in_specs,
        out_specs=[tok, s_spec, v_spec] + cast_out_specs,
        out_shape=[jax.ShapeDtypeStruct((tokens, d), F32),
                   jax.ShapeDtypeStruct(state_ret.shape, F32),
                   jax.ShapeDtypeStruct((steps, seqs, RET_V), F32)] + cast_out_shapes,
        input_output_aliases=aliases,
        compiler_params=pltpu.CompilerParams(
            dimension_semantics=("arbitrary",),
            vmem_limit_bytes=V7X_VMEM_BYTES - 8 * MIB),
        name="prompt_ffn_sample_state",
    )(*args)
    return outs[0], outs[1], outs[2].reshape(n, RET_V), tuple(outs[3:])


def _sample_pre_kernel(x_ref, cos_ref, sin_ref, ng_ref, win_ref, bg_ref, lng_ref, lnb_ref,
                       wsd_ref, bsd_ref, v_ref, a_ref, q_ref, k_ref, vr_ref, sg_ref, gate_ref):
    xn = _rms(x_ref[...], ng_ref[...]).astype(BF16)
    u = jax.nn.gelu(_dot(xn, win_ref[:, OFF_U:OFF_U + GM_WIDTH]))
    v = jax.nn.gelu(_dot(xn, win_ref[:, OFF_V:OFF_V + GM_WIDTH]))
    v = _layer_norm(v, lng_ref[...], lnb_ref[...])
    v_ref[...] = v
    a_ref[...] = (u * (wsd_ref[...] * v + bsd_ref[...])).astype(BF16)
    cos = cos_ref[...]
    sin = sin_ref[...]
    for h in range(RET_HEADS):
        qs = slice(h * RET_DK, (h + 1) * RET_DK)
        q = _dot(xn, win_ref[:, OFF_Q + h * RET_DK:OFF_Q + (h + 1) * RET_DK])
        k = _dot(xn, win_ref[:, OFF_K + h * RET_DK:OFF_K + (h + 1) * RET_DK])
        q_ref[:, qs] = _rotate(q, cos, sin)
        k_ref[:, qs] = _rotate(k, cos, sin) * (RET_DK ** -0.5)
    vr_ref[...] = _dot(xn, win_ref[:, OFF_VR:OFF_VR + RET_V])
    sg_ref[...] = jax.nn.silu(_dot(xn, win_ref[:, OFF_GR:OFF_GR + RET_V]))
    gate_ref[...] = jax.nn.sigmoid(_dot(xn, win_ref[:, OFF_GA:OFF_GA + 2 * D_MODEL]) + bg_ref[...])


def _sample_pre_call(x, cos, sin, ng, win, bg, lng, lnb, wsd, bsd):
    n = x.shape[0]
    out_shape = [jax.ShapeDtypeStruct((n, GM_WIDTH), F32),
                 jax.ShapeDtypeStruct((n, GM_WIDTH), BF16),
                 jax.ShapeDtypeStruct((n, RET_QK), F32),
                 jax.ShapeDtypeStruct((n, RET_QK), F32),
                 jax.ShapeDtypeStruct((n, RET_V), F32),
                 jax.ShapeDtypeStruct((n, RET_V), F32),
                 jax.ShapeDtypeStruct((n, 2 * D_MODEL), F32)]
    args = [x, cos, sin, ng, win, bg, lng, lnb, wsd, bsd]
    return pl.pallas_call(
        _sample_pre_kernel, grid=(1,), in_specs=[_full(a.shape) for a in args],
        out_specs=[_full(s.shape) for s in out_shape], out_shape=out_shape,
        compiler_params=pltpu.CompilerParams(
            dimension_semantics=("arbitrary",), vmem_limit_bytes=V7X_VMEM_BYTES - 16 * MIB),
        name="sample_pre",
    )(*args)


def _sample_post_kernel(x_ref, o_ref, sg_ref, gate_ref, a_ref, wpg_ref, wpr_ref, wo_ref,
                        nfg_ref, wup_ref, wdn_ref, gf_ref, y_ref, *, final):
    rs = []
    for h in range(RET_HEADS):
        hs = slice(h * RET_DV, (h + 1) * RET_DV)
        o = o_ref[:, hs]
        on = o * lax.rsqrt(jnp.mean(o * o, axis=-1, keepdims=True) + EPS)
        rs.append((sg_ref[:, hs] * on).astype(BF16))
    r = jnp.concatenate(rs, axis=-1)
    hm = (gate_ref[:, :D_MODEL] * _dot(a_ref[...], wpg_ref[...])
          + gate_ref[:, D_MODEL:] * _dot(r, wpr_ref[...]))
    x = x_ref[...] + _dot(hm.astype(BF16), wo_ref[...])
    y = _ffn_body(x, nfg_ref[...], wup_ref, wdn_ref)
    if final:
        y = _rms(y, gf_ref[...])
    y_ref[...] = y


def _sample_post_call(x, o, sg, gate, a, wpg, wpr, wo, nfg, wup, wdn, gf, final):
    args = [x, o, sg, gate, a, wpg, wpr, wo, nfg, wup, wdn, gf]
    return pl.pallas_call(
        functools.partial(_sample_post_kernel, final=final),
        grid=(1,), in_specs=[_full(v.shape) for v in args], out_specs=_full(x.shape),
        out_shape=jax.ShapeDtypeStruct(x.shape, F32),
        compiler_params=pltpu.CompilerParams(
            dimension_semantics=("arbitrary",), vmem_limit_bytes=V7X_VMEM_BYTES - 16 * MIB),
        name="sample_post",
    )(*args)


def kernel(x_prompt, x_sample, state_ret, norm_mix_g, w_in, b_gate, gm_ln_g, gm_ln_b, gm_w_s, gm_b_s,
           w_proj_gm, w_proj_ret, w_out, norm_ffn_g, w_up, w_down, norm_final_g):
    b, l, d = x_prompt.shape
    n = x_sample.shape[0]
    assert l % MIX_TILE == 0 and MIX_TILE % RET_CHUNK == 0 and x_sample.shape[1] == 1

    intra, qdec, kdec, chunk_decay = _retention_constants(RET_CHUNK)
    cos_p, sin_p = _rope_tables(0, l)
    cos_s, sin_s = _rope_tables(PAST_LEN, 1)
    row = lambda a: a.reshape(1, -1)
    gf = row(norm_final_g)

    mixer_f32 = (w_in, w_proj_gm, w_proj_ret, w_out)
    ffn_f32 = (w_up, w_down)
    win, wpg, wpr, wo = (w[0].astype(BF16) for w in mixer_f32)

    xp = x_prompt
    xs = x_sample.reshape(n, d)
    v_rows = []
    prompt_states = None
    sample_states = None
    for layer in range(DEPTH):
        ng = row(norm_mix_g[layer])
        bg = row(b_gate[layer])
        lng = row(gm_ln_g[layer])
        lnb = row(gm_ln_b[layer])
        nfg = row(norm_ffn_g[layer])
        final = layer == DEPTH - 1

        wsd = row(jnp.repeat(gm_w_s[layer, :, 0, 0], GM_GROUP_DIM))
        bsd = row(jnp.repeat(gm_b_s[layer, :, 0], GM_GROUP_DIM))
        v_s, a_s, q_s, k_s, vr_s, sg_s, gate_s = _sample_pre_call(
            xs, cos_s, sin_s, ng, win, bg, lng, lnb, wsd, bsd)
        v_rows.append(v_s)

        xp, prompt_states, (wup, wdn) = _mixer_call(
            layer, xp, cos_p, sin_p, ng, win, bg, lng, lnb, gm_w_s[layer], gm_b_s[layer].T,
            intra, qdec, kdec, wpg, wpr, wo, chunk_decay, prompt_states, ffn_f32)
        xp, sample_states, o_s, next_mixer = _ffn_state_call(
            layer, xp.reshape(b * l, d), nfg, wup, wdn, gf, final, state_ret, q_s, k_s, vr_s,
            sample_states, () if final else mixer_f32)
        xp = xp.reshape(b, l, d)
        xs = _sample_post_call(xs, o_s, sg_s, gate_s, a_s, wpg, wpr, wo, nfg, wup, wdn, gf, final)
        if not final:
            win, wpg, wpr, wo = next_mixer

    return (xp, xs.reshape(n, 1, d), prompt_states, sample_states,
            jnp.stack(v_rows).reshape(DEPTH, n, 1, GM_WIDTH))
```

```python
import functools

import numpy as np
import jax
import jax.numpy as jnp
from jax import lax
from jax.experimental import pallas as pl
from jax.experimental.pallas import tpu as pltpu

D_MODEL = 1024
DEPTH = 4
PAST_LEN = 16384
GM_WIDTH = D_MODEL
GM_GROUPS = 4
GM_GROUP_DIM = GM_WIDTH // GM_GROUPS
GM_CHUNK = 128
RET_HEADS = 4
RET_DK = D_MODEL // RET_HEADS
RET_DV = 2 * RET_DK
RET_QK = RET_HEADS * RET_DK
RET_V = RET_HEADS * RET_DV
ROPE_BASE = 10000.0
D_FF = 4 * D_MODEL
EPS = 1e-6

OFF_U = 0
OFF_V = GM_WIDTH
OFF_Q = 2 * GM_WIDTH
OFF_K = OFF_Q + RET_QK
OFF_VR = OFF_K + RET_QK
OFF_GR = OFF_VR + RET_V
OFF_GA = OFF_GR + RET_V
OFF_GB = OFF_GA + D_MODEL
IN_WIDTH = OFF_GB + D_MODEL

V7X_VMEM_BYTES = 64 * 1024 * 1024
MIB = 1024 * 1024

MIX_TILE = 512
RET_CHUNK = 256
FFN_TILE = 256
FF_SPLIT = 1024

BF16 = jnp.bfloat16
F32 = jnp.float32


def _dot(a, b):
    return jnp.dot(a, b, preferred_element_type=F32)


def _dot_nt(a, b):
    return lax.dot_general(a, b, (((1,), (1,)), ((), ())), preferred_element_type=F32)


def _dot_tn(a, b):
    return lax.dot_general(a, b, (((0,), (0,)), ((), ())), preferred_element_type=F32)


def _rms(x, g):
    return x * lax.rsqrt(jnp.mean(x * x, axis=-1, keepdims=True) + EPS) * g


def _layer_norm(x, g, b):
    mu = jnp.mean(x, axis=-1, keepdims=True)
    xc = x - mu
    return xc * lax.rsqrt(jnp.mean(xc * xc, axis=-1, keepdims=True) + EPS) * g + b


def _rotate(x, cos, sin):
    half = RET_DK // 2
    x1 = x[:, :half]
    x2 = x[:, half:]
    return jnp.concatenate([x1 * cos - x2 * sin, x1 * sin + x2 * cos], axis=-1)


def _head_gammas():
    return tuple(1.0 - 2.0 ** (-5.0 - h) for h in range(RET_HEADS))


def _full(shape):
    n = len(shape)
    return pl.BlockSpec(shape, lambda *_: (0,) * n)


def _cast_specs(stacked, layer, steps, step_of):
    _, rows, cols = stacked.shape
    r = rows // steps
    assert r * steps == rows
    src = pl.BlockSpec((None, r, cols), lambda *idx: (layer, step_of(*idx), 0))
    dst = pl.BlockSpec((r, cols), lambda *idx: (step_of(*idx), 0))
    return src, dst


def _cast_rows(cast_in, cast_out):
    for src, dst in zip(cast_in, cast_out):
        dst[...] = src[...].astype(BF16)


def _mixer_kernel(*refs, chunk_decay, n_cast):
    (x_ref, cos_ref, sin_ref, ng_ref, win_ref, bg_ref, lng_ref, lnb_ref, ws_ref, bst_ref,
     intra_ref, qdec_ref, kdec_ref, wpg_ref, wpr_ref, wo_ref) = refs[:16]
    cast_in = refs[16:16 + n_cast]
    y_ref, s_ref = refs[-4 - n_cast:-2 - n_cast]
    cast_out = refs[-2 - n_cast:-2]
    a_scr, r_scr = refs[-2:]
    c = RET_CHUNK
    n_chunks = x_ref.shape[0] // c

    @pl.when(pl.program_id(1) == 0)
    def _():
        s_ref[...] = jnp.zeros(s_ref.shape, F32)

    row = lax.broadcasted_iota(jnp.int32, (GM_CHUNK, GM_CHUNK), 0)
    col = lax.broadcasted_iota(jnp.int32, (GM_CHUNK, GM_CHUNK), 1)

    def chunk_body(i, carry):
        rows = pl.ds(i * c, c)
        x = x_ref[rows, :]
        xn = _rms(x, ng_ref[...]).astype(BF16)

        u = jax.nn.gelu(_dot(xn, win_ref[:, OFF_U:OFF_U + GM_WIDTH]))
        v = jax.nn.gelu(_dot(xn, win_ref[:, OFF_V:OFF_V + GM_WIDTH]))
        ga = jax.nn.sigmoid(_dot(xn, win_ref[:, OFF_GA:OFF_GA + D_MODEL]) + bg_ref[:, :D_MODEL])
        gb = jax.nn.sigmoid(_dot(xn, win_ref[:, OFF_GB:OFF_GB + D_MODEL]) + bg_ref[:, D_MODEL:])

        vb = _layer_norm(v, lng_ref[...], lnb_ref[...]).astype(BF16)
        for g in range(GM_GROUPS):
            w_tril = jnp.where(row >= col, ws_ref[g], 0.0).astype(BF16)
            b_col = bst_ref[:, g:g + 1]
            cs = slice(g * GM_GROUP_DIM, (g + 1) * GM_GROUP_DIM)
            for k in range(c // GM_CHUNK):
                rs = slice(k * GM_CHUNK, (k + 1) * GM_CHUNK)
                mixed = _dot(w_tril, vb[rs, cs]) + b_col
                a_scr[rs, cs] = (u[rs, cs] * mixed).astype(BF16)

        cos = cos_ref[rows, :]
        sin = sin_ref[rows, :]
        for h in range(RET_HEADS):
            q = _dot(xn, win_ref[:, OFF_Q + h * RET_DK:OFF_Q + (h + 1) * RET_DK])
            k = _dot(xn, win_ref[:, OFF_K + h * RET_DK:OFF_K + (h + 1) * RET_DK])
            vr = _dot(xn, win_ref[:, OFF_VR + h * RET_DV:OFF_VR + (h + 1) * RET_DV]).astype(BF16)
            gr = _dot(xn, win_ref[:, OFF_GR + h * RET_DV:OFF_GR + (h + 1) * RET_DV])
            qr = _rotate(q, cos, sin).astype(BF16)
            kr = _rotate(k, cos, sin) * (RET_DK ** -0.5)
            s_old = s_ref[h]
            scores = _dot_nt(qr, kr.astype(BF16)) * intra_ref[h]
            inner = _dot(scores.astype(BF16), vr)
            cross = _dot(qr, s_old.astype(BF16)) * qdec_ref[h]
            o = inner + cross
            s_ref[h] = s_old * chunk_decay[h] + _dot_tn((kr * kdec_ref[h]).astype(BF16), vr)
            on = o * lax.rsqrt(jnp.mean(o * o, axis=-1, keepdims=True) + EPS)
            r_scr[:, h * RET_DV:(h + 1) * RET_DV] = (jax.nn.silu(gr) * on).astype(BF16)

        hm = ga * _dot(a_scr[...], wpg_ref[...]) + gb * _dot(r_scr[...], wpr_ref[...])
        y_ref[rows, :] = x + _dot(hm.astype(BF16), wo_ref[...])

        for src, dst in zip(cast_in, cast_out):
            r = src.shape[0] // n_chunks
            part = pl.ds(i * r, r)
            dst[part, :] = src[part, :].astype(BF16)
        return carry

    for i in range(n_chunks):
        chunk_body(i, 0)


def _retention_constants(c):
    log_g = np.log(np.array(_head_gammas(), np.float64))
    idx = np.arange(c, dtype=np.float64)
    diff = idx[:, None] - idx[None, :]
    intra = np.where(diff[None] >= 0, np.exp(diff[None] * log_g[:, None, None]), 0.0)
    qdec = np.exp((idx[None, :, None] + 1.0) * log_g[:, None, None])
    kdec = np.exp((c - 1.0 - idx[None, :, None]) * log_g[:, None, None])
    chunk_decay = tuple(float(v) for v in np.exp(c * log_g).astype(np.float32))
    return (jnp.asarray(intra, F32), jnp.asarray(qdec, F32), jnp.asarray(kdec, F32), chunk_decay)


def _rope_tables(pos0, length):
    inv_freq = 1.0 / (ROPE_BASE ** jnp.linspace(0.0, 1.0, RET_DK // 2, dtype=F32))
    pos = (jnp.arange(length, dtype=jnp.int32) + pos0).astype(F32)
    ang = pos[:, None] * inv_freq[None, :]
    return jnp.cos(ang), jnp.sin(ang)


def _mixer_call(layer, x, cos, sin, ng, win, bg, lng, lnb, ws, bst, intra, qdec, kdec, wpg, wpr,
                wo, chunk_decay, prev_states, cast_srcs):
    b, l, d = x.shape
    t = MIX_TILE
    per_row = l // t
    steps = b * per_row
    tok = pl.BlockSpec((None, t, d), lambda i, j: (i, j, 0))
    rope = pl.BlockSpec((t, RET_DK // 2), lambda i, j: (j, 0))
    in_specs = [tok, rope, rope, _full(ng.shape), _full(win.shape), _full(bg.shape),
                _full(lng.shape), _full(lnb.shape), _full(ws.shape), _full(bst.shape),
                _full(intra.shape), _full(qdec.shape), _full(kdec.shape),
                _full(wpg.shape), _full(wpr.shape), _full(wo.shape)]
    args = [x, cos, sin, ng, win, bg, lng, lnb, ws, bst, intra, qdec, kdec, wpg, wpr, wo]
    cast_out_specs = []
    cast_out_shapes = []
    for w in cast_srcs:
        src, dst = _cast_specs(w, layer, steps, lambda i, j: i * per_row + j)
        in_specs.append(src)
        args.append(w)
        cast_out_specs.append(dst)
        cast_out_shapes.append(jax.ShapeDtypeStruct(w.shape[1:], BF16))
    aliases = {}
    if prev_states is not None:
        in_specs.append(pl.BlockSpec(memory_space=pl.ANY))
        args.append(prev_states)
        aliases = {len(args) - 1: 1}
    state_spec = pl.BlockSpec((None, None, RET_HEADS, RET_DK, RET_DV),
                              lambda i, j: (layer, i, 0, 0, 0))
    outs = pl.pallas_call(
        functools.partial(_mixer_kernel, chunk_decay=chunk_decay, n_cast=len(cast_srcs)),
        grid=(b, per_row), in_specs=in_specs,
        out_specs=[tok, state_spec] + cast_out_specs,
        out_shape=[jax.ShapeDtypeStruct((b, l, d), F32),
                   jax.ShapeDtypeStruct((DEPTH, b, RET_HEADS, RET_DK, RET_DV), F32)]
        + cast_out_shapes,
        scratch_shapes=[pltpu.VMEM((RET_CHUNK, GM_WIDTH), BF16),
                        pltpu.VMEM((RET_CHUNK, RET_V), BF16)],
        input_output_aliases=aliases,
        compiler_params=pltpu.CompilerParams(
            dimension_semantics=("arbitrary", "arbitrary"),
            vmem_limit_bytes=V7X_VMEM_BYTES - 6 * MIB),
        name="prompt_mixer",
    )(*args)
    return outs[0], outs[1], tuple(outs[2:])


def _ffn_body(x, hn, wup_ref, wdn_ref, anchor=None):
    acc = x
    for j in range(D_FF // FF_SPLIT):
        cols = slice(j * FF_SPLIT, (j + 1) * FF_SPLIT)
        up = _dot(hn, wup_ref[:, cols])
        if anchor is not None:
            up = up + anchor[:, cols]
        act = jnp.square(jnp.maximum(up, 0.0)).astype(BF16)
        acc = acc + _dot(act, wdn_ref[cols, :])
    return acc


def _zero_of(v):
    bits = pltpu.bitcast(v, jnp.int32)
    bits = lax.shift_right_logical(lax.shift_right_logical(bits, 16), 16)
    return bits.astype(F32)


def _ffn_state_kernel(*refs, final, gammas, seqs, n_cast):
    x_ref, g_ref, wup_ref, wdn_ref, gf_ref, s_ref, qk_ref, v_ref = refs[:8]
    cast_in = refs[8:8 + n_cast]
    y_ref, snew_ref, o_ref = refs[-3 - n_cast:len(refs) - n_cast]
    cast_out = refs[len(refs) - n_cast:]

    _cast_rows(cast_in, cast_out)

    zeros = []
    for j in range(seqs):
        for h in range(RET_HEADS):
            q_col = qk_ref[:, j * RET_HEADS + h:j * RET_HEADS + h + 1]
            k_col = qk_ref[:, (seqs + j) * RET_HEADS + h:(seqs + j) * RET_HEADS + h + 1]
            v_row = v_ref[j:j + 1, h * RET_DV:(h + 1) * RET_DV]
            snew_ref[j, h] = s_ref[j, h] * gammas[h] + k_col * v_row
            o = jnp.sum(q_col * snew_ref[j, h], axis=0, keepdims=True)
            o_ref[j:j + 1, h * RET_DV:(h + 1) * RET_DV] = o
            zeros.append(_zero_of(o))

    anchor = jnp.concatenate(zeros, axis=-1)
    assert anchor.shape == (1, D_FF)
    x = x_ref[...]
    y = _ffn_body(x, _rms(x, g_ref[...]).astype(BF16), wup_ref, wdn_ref, anchor)
    if final:
        y = _rms(y, gf_ref[...])
    y_ref[...] = y


def _ffn_state_call(layer, x2d, g, wup, wdn, gf, final, state_ret, q_s, k_s, v_s, prev_out,
                    cast_srcs):
    tokens, d = x2d.shape
    n = q_s.shape[0]
    t = FFN_TILE
    steps = tokens // t
    seqs = n // steps
    assert steps * t == tokens and seqs * steps == n

    q4 = q_s.reshape(steps, seqs, RET_HEADS, RET_DK)
    k4 = k_s.reshape(steps, seqs, RET_HEADS, RET_DK)
    qk = jnp.stack([q4, k4], axis=0).transpose(1, 4, 0, 2, 3).reshape(
        steps, RET_DK, 2 * seqs * RET_HEADS)
    v3 = v_s.reshape(steps, seqs, RET_V)

    tok = pl.BlockSpec((t, d), lambda i: (i, 0))
    s_spec = pl.BlockSpec((None, seqs, RET_HEADS, RET_DK, RET_DV), lambda i: (layer, i, 0, 0, 0))
    qk_spec = pl.BlockSpec((None, RET_DK, 2 * seqs * RET_HEADS), lambda i: (i, 0, 0))
    v_spec = pl.BlockSpec((None, seqs, RET_V), lambda i: (i, 0, 0))
    in_specs = [tok, _full(g.shape), _full(wup.shape), _full(wdn.shape), _full(gf.shape),
                s_spec, qk_spec, v_spec]
    args = [x2d, g, wup, wdn, gf, state_ret, qk, v3]
    cast_out_specs = []
    cast_out_shapes = []
    for w in cast_srcs:
        src, dst = _cast_specs(w, layer + 1, steps, lambda i: i)
        in_specs.append(src)
        args.append(w)
        cast_out_specs.append(dst)
        cast_out_shapes.append(jax.ShapeDtypeStruct(w.shape[1:], BF16))
    aliases = {}
    if prev_out is not None:
        in_specs.append(pl.BlockSpec(memory_space=pl.ANY))
        args.append(prev_out)
        aliases = {len(args) - 1: 1}
    outs = pl.pallas_call(
        functools.partial(_ffn_state_kernel, final=final, gammas=_head_gammas(), seqs=seqs,
                          n_cast=len(cast_srcs)),
        grid=(steps,), in_specs=in_specs,
        out_specs=[tok, s_spec, v_spec] + cast_out_specs,
        out_shape=[jax.ShapeDtypeStruct((tokens, d), F32),
                   jax.ShapeDtypeStruct(state_ret.shape, F32),
                   jax.ShapeDtypeStruct((steps, seqs, RET_V), F32)] + cast_out_shapes,
        input_output_aliases=aliases,
        compiler_params=pltpu.CompilerParams(
            dimension_semantics=("arbitrary",),
            vmem_limit_bytes=V7X_VMEM_BYTES - 8 * MIB),
        name="prompt_ffn_sample_state",
    )(*args)
    return outs[0], outs[1], outs[2].reshape(n, RET_V), tuple(outs[3:])


def _sample_pre_kernel(x_ref, cos_ref, sin_ref, ng_ref, win_ref, bg_ref, lng_ref, lnb_ref,
                       wsd_ref, bsd_ref, v_ref, a_ref, q_ref, k_ref, vr_ref, sg_ref, gate_ref):
    xn = _rms(x_ref[...], ng_ref[...]).astype(BF16)
    u = jax.nn.gelu(_dot(xn, win_ref[:, OFF_U:OFF_U + GM_WIDTH]))
    v = jax.nn.gelu(_dot(xn, win_ref[:, OFF_V:OFF_V + GM_WIDTH]))
    v = _layer_norm(v, lng_ref[...], lnb_ref[...])
    v_ref[...] = v
    a_ref[...] = (u * (wsd_ref[...] * v + bsd_ref[...])).astype(BF16)
    cos = cos_ref[...]
    sin = sin_ref[...]
    for h in range(RET_HEADS):
        qs = slice(h * RET_DK, (h + 1) * RET_DK)
        q = _dot(xn, win_ref[:, OFF_Q + h * RET_DK:OFF_Q + (h + 1) * RET_DK])
        k = _dot(xn, win_ref[:, OFF_K + h * RET_DK:OFF_K + (h + 1) * RET_DK])
        q_ref[:, qs] = _rotate(q, cos, sin)
        k_ref[:, qs] = _rotate(k, cos, sin) * (RET_DK ** -0.5)
    vr_ref[...] = _dot(xn, win_ref[:, OFF_VR:OFF_VR + RET_V])
    sg_ref[...] = jax.nn.silu(_dot(xn, win_ref[:, OFF_GR:OFF_GR + RET_V]))
    gate_ref[...] = jax.nn.sigmoid(_dot(xn, win_ref[:, OFF_GA:OFF_GA + 2 * D_MODEL]) + bg_ref[...])


def _sample_pre_call(x, cos, sin, ng, win, bg, lng, lnb, wsd, bsd):
    n = x.shape[0]
    out_shape = [jax.ShapeDtypeStruct((n, GM_WIDTH), F32),
                 jax.ShapeDtypeStruct((n, GM_WIDTH), BF16),
                 jax.ShapeDtypeStruct((n, RET_QK), F32),
                 jax.ShapeDtypeStruct((n, RET_QK), F32),
                 jax.ShapeDtypeStruct((n, RET_V), F32),
                 jax.ShapeDtypeStruct((n, RET_V), F32),
                 jax.ShapeDtypeStruct((n, 2 * D_MODEL), F32)]
    args = [x, cos, sin, ng, win, bg, lng, lnb, wsd, bsd]
    return pl.pallas_call(
        _sample_pre_kernel, grid=(1,), in_specs=[_full(a.shape) for a in args],
        out_specs=[_full(s.shape) for s in out_shape], out_shape=out_shape,
        compiler_params=pltpu.CompilerParams(
            dimension_semantics=("arbitrary",), vmem_limit_bytes=V7X_VMEM_BYTES - 16 * MIB),
        name="sample_pre",
    )(*args)


def _sample_post_kernel(x_ref, o_ref, sg_ref, gate_ref, a_ref, wpg_ref, wpr_ref, wo_ref,
                        nfg_ref, wup_ref, wdn_ref, gf_ref, y_ref, *, final):
    rs = []
    for h in range(RET_HEADS):
        hs = slice(h * RET_DV, (h + 1) * RET_DV)
        o = o_ref[:, hs]
        on = o * lax.rsqrt(jnp.mean(o * o, axis=-1, keepdims=True) + EPS)
        rs.append((sg_ref[:, hs] * on).astype(BF16))
    r = jnp.concatenate(rs, axis=-1)
    hm = (gate_ref[:, :D_MODEL] * _dot(a_ref[...], wpg_ref[...])
          + gate_ref[:, D_MODEL:] * _dot(r, wpr_ref[...]))
    x = x_ref[...] + _dot(hm.astype(BF16), wo_ref[...])
    y = _ffn_body(x, _rms(x, nfg_ref[...]).astype(BF16), wup_ref, wdn_ref)
    if final:
        y = _rms(y, gf_ref[...])
    y_ref[...] = y


def _sample_post_call(x, o, sg, gate, a, wpg, wpr, wo, nfg, wup, wdn, gf, final):
    args = [x, o, sg, gate, a, wpg, wpr, wo, nfg, wup, wdn, gf]
    return pl.pallas_call(
        functools.partial(_sample_post_kernel, final=final),
        grid=(1,), in_specs=[_full(v.shape) for v in args], out_specs=_full(x.shape),
        out_shape=jax.ShapeDtypeStruct(x.shape, F32),
        compiler_params=pltpu.CompilerParams(
            dimension_semantics=("arbitrary",), vmem_limit_bytes=V7X_VMEM_BYTES - 16 * MIB),
        name="sample_post",
    )(*args)


def kernel(x_prompt, x_sample, state_ret, norm_mix_g, w_in, b_gate, gm_ln_g, gm_ln_b, gm_w_s, gm_b_s,
           w_proj_gm, w_proj_ret, w_out, norm_ffn_g, w_up, w_down, norm_final_g):
    b, l, d = x_prompt.shape
    n = x_sample.shape[0]
    assert l % MIX_TILE == 0 and MIX_TILE % RET_CHUNK == 0 and x_sample.shape[1] == 1

    intra, qdec, kdec, chunk_decay = _retention_constants(RET_CHUNK)
    cos_p, sin_p = _rope_tables(0, l)
    cos_s, sin_s = _rope_tables(PAST_LEN, 1)
    row = lambda a: a.reshape(1, -1)
    gf = row(norm_final_g)

    mixer_f32 = (w_in, w_proj_gm, w_proj_ret, w_out)
    ffn_f32 = (w_up, w_down)
    win, wpg, wpr, wo = (w[0].astype(BF16) for w in mixer_f32)

    xp = x_prompt
    xs = x_sample.reshape(n, d)
    v_rows = []
    prompt_states = None
    sample_states = None
    for layer in range(DEPTH):
        ng = row(norm_mix_g[layer])
        bg = row(b_gate[layer])
        lng = row(gm_ln_g[layer])
        lnb = row(gm_ln_b[layer])
        nfg = row(norm_ffn_g[layer])
        final = layer == DEPTH - 1

        wsd = row(jnp.repeat(gm_w_s[layer, :, 0, 0], GM_GROUP_DIM))
        bsd = row(jnp.repeat(gm_b_s[layer, :, 0], GM_GROUP_DIM))
        v_s, a_s, q_s, k_s, vr_s, sg_s, gate_s = _sample_pre_call(
            xs, cos_s, sin_s, ng, win, bg, lng, lnb, wsd, bsd)
        v_rows.append(v_s)

        xp, prompt_states, (wup, wdn) = _mixer_call(
            layer, xp, cos_p, sin_p, ng, win, bg, lng, lnb, gm_w_s[layer], gm_b_s[layer].T,
            intra, qdec, kdec, wpg, wpr, wo, chunk_decay, prompt_states, ffn_f32)
        xp, sample_states, o_s, next_mixer = _ffn_state_call(
            layer, xp.reshape(b * l, d), nfg, wup, wdn, gf, final, state_ret, q_s, k_s, vr_s,
            sample_states, () if final else mixer_f32)
        xp = xp.reshape(b, l, d)
        xs = _sample_post_call(xs, o_s, sg_s, gate_s, a_s, wpg, wpr, wo, nfg, wup, wdn, gf, final)
        if not final:
            win, wpg, wpr, wo = next_mixer

    return (xp, xs.reshape(n, 1, d), prompt_states, sample_states,
            jnp.stack(v_rows).reshape(DEPTH, n, 1, GM_WIDTH))
```

```python
import functools

import numpy as np
import jax
import jax.numpy as jnp
from jax import lax
from jax.experimental import pallas as pl
from jax.experimental.pallas import tpu as pltpu

D_MODEL = 1024
DEPTH = 4
PAST_LEN = 16384
GM_WIDTH = D_MODEL
GM_GROUPS = 4
GM_GROUP_DIM = GM_WIDTH // GM_GROUPS
GM_CHUNK = 128
RET_HEADS = 4
RET_DK = D_MODEL // RET_HEADS
RET_DV = 2 * RET_DK
RET_QK = RET_HEADS * RET_DK
RET_V = RET_HEADS * RET_DV
ROPE_BASE = 10000.0
D_FF = 4 * D_MODEL
EPS = 1e-6

OFF_U = 0
OFF_V = GM_WIDTH
OFF_Q = 2 * GM_WIDTH
OFF_K = OFF_Q + RET_QK
OFF_VR = OFF_K + RET_QK
OFF_GR = OFF_VR + RET_V
OFF_GA = OFF_GR + RET_V
OFF_GB = OFF_GA + D_MODEL
IN_WIDTH = OFF_GB + D_MODEL

V7X_VMEM_BYTES = 64 * 1024 * 1024
V7X_LANES = 128
V7X_SUBLANES = 8

MIX_TILE = 512
RET_CHUNK = 256
FFN_TILE = 256
FF_SPLIT = 1024

BF16 = jnp.bfloat16
F32 = jnp.float32


def _dot(a, b):
    return jnp.dot(a, b, preferred_element_type=F32)


def _dot_nt(a, b):
    return lax.dot_general(a, b, (((1,), (1,)), ((), ())), preferred_element_type=F32)


def _dot_tn(a, b):
    return lax.dot_general(a, b, (((0,), (0,)), ((), ())), preferred_element_type=F32)


def _rms(x, g):
    return x * lax.rsqrt(jnp.mean(x * x, axis=-1, keepdims=True) + EPS) * g


def _layer_norm(x, g, b):
    mu = jnp.mean(x, axis=-1, keepdims=True)
    xc = x - mu
    return xc * lax.rsqrt(jnp.mean(xc * xc, axis=-1, keepdims=True) + EPS) * g + b


def _rotate(x, cos, sin):
    half = RET_DK // 2
    x1 = x[:, :half]
    x2 = x[:, half:]
    return jnp.concatenate([x1 * cos - x2 * sin, x1 * sin + x2 * cos], axis=-1)


def _head_gammas():
    return tuple(1.0 - 2.0 ** (-5.0 - h) for h in range(RET_HEADS))


def _full(shape):
    n = len(shape)
    return pl.BlockSpec(shape, lambda *_: (0,) * n)


def _window_bytes(shape, dtype):
    itemsize = jnp.dtype(dtype).itemsize
    dims = [d for d in shape if d is not None]
    dims[-1] = -(-dims[-1] // V7X_LANES) * V7X_LANES
    if len(dims) > 1:
        sub = V7X_SUBLANES * 4 // itemsize
        dims[-2] = -(-dims[-2] // sub) * sub
    return int(np.prod(dims)) * itemsize


def _vmem_limit(whole, streamed, temp_tiles):
    need = (sum(_window_bytes(s, d) for s, d in whole)
            + 2 * sum(_window_bytes(s, d) for s, d in streamed)
            + temp_tiles * _window_bytes((RET_CHUNK, D_MODEL), F32))
    assert need <= V7X_VMEM_BYTES, need
    return need


def _cast_specs(stacked, layer, steps, step_of):
    _, rows, cols = stacked.shape
    r = rows // steps
    assert r * steps == rows
    src = pl.BlockSpec((None, r, cols), lambda *idx: (layer, step_of(*idx), 0))
    dst = pl.BlockSpec((r, cols), lambda *idx: (step_of(*idx), 0))
    return src, dst


def _cast_rows(cast_in, cast_out):
    for src, dst in zip(cast_in, cast_out):
        dst[...] = src[...].astype(BF16)


def _mixer_kernel(*refs, chunk_decay, n_cast):
    (x_ref, cos_ref, sin_ref, ng_ref, win_ref, bg_ref, lng_ref, lnb_ref, ws_ref, bst_ref,
     intra_ref, qdec_ref, kdec_ref, wpg_ref, wpr_ref, wo_ref) = refs[:16]
    cast_in = refs[16:16 + n_cast]
    y_ref, s_ref = refs[-4 - n_cast:-2 - n_cast]
    cast_out = refs[-2 - n_cast:-2]
    a_scr, r_scr = refs[-2:]
    c = RET_CHUNK
    n_chunks = x_ref.shape[0] // c

    @pl.when(pl.program_id(1) == 0)
    def _():
        s_ref[...] = jnp.zeros(s_ref.shape, F32)

    row = lax.broadcasted_iota(jnp.int32, (GM_CHUNK, GM_CHUNK), 0)
    col = lax.broadcasted_iota(jnp.int32, (GM_CHUNK, GM_CHUNK), 1)

    def chunk_body(i):
        rows = pl.ds(i * c, c)
        x = x_ref[rows, :]
        xn = _rms(x, ng_ref[...]).astype(BF16)

        u = jax.nn.gelu(_dot(xn, win_ref[:, OFF_U:OFF_U + GM_WIDTH]))
        v = jax.nn.gelu(_dot(xn, win_ref[:, OFF_V:OFF_V + GM_WIDTH]))
        ga = jax.nn.sigmoid(_dot(xn, win_ref[:, OFF_GA:OFF_GA + D_MODEL]) + bg_ref[:, :D_MODEL])
        gb = jax.nn.sigmoid(_dot(xn, win_ref[:, OFF_GB:OFF_GB + D_MODEL]) + bg_ref[:, D_MODEL:])

        vb = _layer_norm(v, lng_ref[...], lnb_ref[...]).astype(BF16)
        for g in range(GM_GROUPS):
            w_tril = jnp.where(row >= col, ws_ref[g], 0.0).astype(BF16)
            b_col = bst_ref[:, g:g + 1]
            cs = slice(g * GM_GROUP_DIM, (g + 1) * GM_GROUP_DIM)
            for k in range(c // GM_CHUNK):
                rs = slice(k * GM_CHUNK, (k + 1) * GM_CHUNK)
                mixed = _dot(w_tril, vb[rs, cs]) + b_col
                a_scr[rs, cs] = (u[rs, cs] * mixed).astype(BF16)

        cos = cos_ref[rows, :]
        sin = sin_ref[rows, :]
        for h in range(RET_HEADS):
            q = _dot(xn, win_ref[:, OFF_Q + h * RET_DK:OFF_Q + (h + 1) * RET_DK])
            k = _dot(xn, win_ref[:, OFF_K + h * RET_DK:OFF_K + (h + 1) * RET_DK])
            vr = _dot(xn, win_ref[:, OFF_VR + h * RET_DV:OFF_VR + (h + 1) * RET_DV]).astype(BF16)
            gr = _dot(xn, win_ref[:, OFF_GR + h * RET_DV:OFF_GR + (h + 1) * RET_DV])
            qr = _rotate(q, cos, sin).astype(BF16)
            kr = _rotate(k, cos, sin) * (RET_DK ** -0.5)
            s_old = s_ref[h]
            scores = _dot_nt(qr, kr.astype(BF16)) * intra_ref[h]
            cross = _dot(qr, s_old.astype(BF16)) * qdec_ref[h]
            s_ref[h] = s_old * chunk_decay[h] + _dot_tn((kr * kdec_ref[h]).astype(BF16), vr)
            o = _dot(scores.astype(BF16), vr) + cross
            on = o * lax.rsqrt(jnp.mean(o * o, axis=-1, keepdims=True) + EPS)
            r_scr[:, h * RET_DV:(h + 1) * RET_DV] = (jax.nn.silu(gr) * on).astype(BF16)

        hm = ga * _dot(a_scr[...], wpg_ref[...]) + gb * _dot(r_scr[...], wpr_ref[...])
        y_ref[rows, :] = x + _dot(hm.astype(BF16), wo_ref[...])

        for src, dst in zip(cast_in, cast_out):
            r = src.shape[0] // n_chunks
            part = pl.ds(i * r, r)
            dst[part, :] = src[part, :].astype(BF16)

    for i in range(n_chunks):
        chunk_body(i)


def _retention_constants(c):
    log_g = np.log(np.array(_head_gammas(), np.float64))
    idx = np.arange(c, dtype=np.float64)
    diff = idx[:, None] - idx[None, :]
    intra = np.where(diff[None] >= 0, np.exp(diff[None] * log_g[:, None, None]), 0.0)
    qdec = np.exp((idx[None, :, None] + 1.0) * log_g[:, None, None])
    kdec = np.exp((c - 1.0 - idx[None, :, None]) * log_g[:, None, None])
    chunk_decay = tuple(float(v) for v in np.exp(c * log_g).astype(np.float32))
    return (jnp.asarray(intra, F32), jnp.asarray(qdec, F32), jnp.asarray(kdec, F32), chunk_decay)


def _rope_tables(pos0, length):
    inv_freq = 1.0 / (ROPE_BASE ** jnp.linspace(0.0, 1.0, RET_DK // 2, dtype=F32))
    pos = (jnp.arange(length, dtype=jnp.int32) + pos0).astype(F32)
    ang = pos[:, None] * inv_freq[None, :]
    return jnp.cos(ang), jnp.sin(ang)


def _mixer_call(layer, x, cos, sin, ng, win, bg, lng, lnb, ws, bst, intra, qdec, kdec, wpg, wpr,
                wo, chunk_decay, prev_states, cast_srcs):
    b, l, d = x.shape
    t = MIX_TILE
    per_row = l // t
    steps = b * per_row
    tok = pl.BlockSpec((None, t, d), lambda i, j: (i, j, 0))
    rope = pl.BlockSpec((t, RET_DK // 2), lambda i, j: (j, 0))
    in_specs = [tok, rope, rope, _full(ng.shape), _full(win.shape), _full(bg.shape),
                _full(lng.shape), _full(lnb.shape), _full(ws.shape), _full(bst.shape),
                _full(intra.shape), _full(qdec.shape), _full(kdec.shape),
                _full(wpg.shape), _full(wpr.shape), _full(wo.shape)]
    args = [x, cos, sin, ng, win, bg, lng, lnb, ws, bst, intra, qdec, kdec, wpg, wpr, wo]
    cast_out_specs = []
    cast_out_shapes = []
    for w in cast_srcs:
        src, dst = _cast_specs(w, layer, steps, lambda i, j: i * per_row + j)
        in_specs.append(src)
        args.append(w)
        cast_out_specs.append(dst)
        cast_out_shapes.append(jax.ShapeDtypeStruct(w.shape[1:], BF16))
    aliases = {}
    if prev_states is not None:
        in_specs.append(pl.BlockSpec(memory_space=pl.ANY))
        args.append(prev_states)
        aliases = {len(args) - 1: 1}
    state_spec = pl.BlockSpec((None, None, RET_HEADS, RET_DK, RET_DV),
                              lambda i, j: (layer, i, 0, 0, 0))
    scratch = [((RET_CHUNK, GM_WIDTH), BF16), ((RET_CHUNK, RET_V), BF16)]
    vmem = _vmem_limit(
        whole=[(a.shape, a.dtype) for a in args[3:16]] + scratch,
        streamed=[((t, d), F32)] * 2 + [((t, RET_DK // 2), F32)] * 2
        + [((RET_HEADS, RET_DK, RET_DV), F32)]
        + [(s.block_shape, dt) for s, dt in zip(in_specs[16:16 + len(cast_srcs)],
                                                 [F32] * len(cast_srcs))]
        + [(s.block_shape, BF16) for s in cast_out_specs],
        temp_tiles=10)
    outs = pl.pallas_call(
        functools.partial(_mixer_kernel, chunk_decay=chunk_decay, n_cast=len(cast_srcs)),
        grid=(b, per_row), in_specs=in_specs,
        out_specs=[tok, state_spec] + cast_out_specs,
        out_shape=[jax.ShapeDtypeStruct((b, l, d), F32),
                   jax.ShapeDtypeStruct((DEPTH, b, RET_HEADS, RET_DK, RET_DV), F32)]
        + cast_out_shapes,
        scratch_shapes=[pltpu.VMEM(s, dt) for s, dt in scratch],
        input_output_aliases=aliases,
        compiler_params=pltpu.CompilerParams(
            dimension_semantics=("arbitrary", "arbitrary"), vmem_limit_bytes=vmem),
        name="prompt_mixer",
    )(*args)
    return outs[0], outs[1], tuple(outs[2:])


def _ffn_body(x, hn, wup_ref, wdn_ref, anchor=None, before_block=None):
    acc = x
    for j in range(D_FF // FF_SPLIT):
        cols = slice(j * FF_SPLIT, (j + 1) * FF_SPLIT)
        if before_block is not None:
            before_block(j)
        up = _dot(hn, wup_ref[:, cols])
        if anchor is not None:
            up = up + anchor[:, cols]
        act = jnp.square(jnp.maximum(up, 0.0)).astype(BF16)
        acc = acc + _dot(act, wdn_ref[cols, :])
    return acc


def _zero_of(v):
    bits = pltpu.bitcast(v, jnp.int32)
    bits = lax.shift_right_logical(lax.shift_right_logical(bits, 16), 16)
    return bits.astype(F32)


def _ffn_state_kernel(*refs, final, gammas, seqs, n_cast):
    x_ref, g_ref, wup_ref, wdn_ref, gf_ref, s_ref, qk_ref, v_ref = refs[:8]
    cast_in = refs[8:8 + n_cast]
    y_ref, snew_ref, o_ref = refs[-3 - n_cast:len(refs) - n_cast]
    cast_out = refs[len(refs) - n_cast:]

    _cast_rows(cast_in, cast_out)

    zeros = []
    for j in range(seqs):
        for h in range(RET_HEADS):
            q_col = qk_ref[:, j * RET_HEADS + h:j * RET_HEADS + h + 1]
            k_col = qk_ref[:, (seqs + j) * RET_HEADS + h:(seqs + j) * RET_HEADS + h + 1]
            v_row = v_ref[j:j + 1, h * RET_DV:(h + 1) * RET_DV]
            snew_ref[j, h] = s_ref[j, h] * gammas[h] + k_col * v_row
            o = jnp.sum(q_col * snew_ref[j, h], axis=0, keepdims=True)
            o_ref[j:j + 1, h * RET_DV:(h + 1) * RET_DV] = o
            zeros.append(_zero_of(o))

    anchor = jnp.concatenate(zeros, axis=-1)
    assert anchor.shape == (1, D_FF)
    x = x_ref[...]
    y = _ffn_body(x, _rms(x, g_ref[...]).astype(BF16), wup_ref, wdn_ref, anchor)
    if final:
        y = _rms(y, gf_ref[...])
    y_ref[...] = y


def _ffn_state_call(layer, x2d, g, wup, wdn, gf, final, state_ret, q_s, k_s, v_s, prev_out,
                    cast_srcs):
    tokens, d = x2d.shape
    n = q_s.shape[0]
    t = FFN_TILE
    steps = tokens // t
    seqs = n // steps
    assert steps * t == tokens and seqs * steps == n

    q4 = q_s.reshape(steps, seqs, RET_HEADS, RET_DK)
    k4 = k_s.reshape(steps, seqs, RET_HEADS, RET_DK)
    qk = jnp.stack([q4, k4], axis=0).transpose(1, 4, 0, 2, 3).reshape(
        steps, RET_DK, 2 * seqs * RET_HEADS)
    v3 = v_s.reshape(steps, seqs, RET_V)

    tok = pl.BlockSpec((t, d), lambda i: (i, 0))
    s_spec = pl.BlockSpec((None, seqs, RET_HEADS, RET_DK, RET_DV), lambda i: (layer, i, 0, 0, 0))
    qk_spec = pl.BlockSpec((None, RET_DK, 2 * seqs * RET_HEADS), lambda i: (i, 0, 0))
    v_spec = pl.BlockSpec((None, seqs, RET_V), lambda i: (i, 0, 0))
    in_specs = [tok, _full(g.shape), _full(wup.shape), _full(wdn.shape), _full(gf.shape),
                s_spec, qk_spec, v_spec]
    args = [x2d, g, wup, wdn, gf, state_ret, qk, v3]
    cast_out_specs = []
    cast_out_shapes = []
    for w in cast_srcs:
        src, dst = _cast_specs(w, layer + 1, steps, lambda i: i)
        in_specs.append(src)
        args.append(w)
        cast_out_specs.append(dst)
        cast_out_shapes.append(jax.ShapeDtypeStruct(w.shape[1:], BF16))
    aliases = {}
    if prev_out is not None:
        in_specs.append(pl.BlockSpec(memory_space=pl.ANY))
        args.append(prev_out)
        aliases = {len(args) - 1: 1}
    vmem = _vmem_limit(
        whole=[(a.shape, a.dtype) for a in (g, wup, wdn, gf)],
        streamed=[(tok.block_shape, F32), (s_spec.block_shape, F32)] * 2
        + [(qk_spec.block_shape, F32)] + [(v_spec.block_shape, F32)] * 2
        + [(s.block_shape, F32) for s in in_specs[8:8 + len(cast_srcs)]]
        + [(s.block_shape, BF16) for s in cast_out_specs],
        temp_tiles=6)
    outs = pl.pallas_call(
        functools.partial(_ffn_state_kernel, final=final, gammas=_head_gammas(), seqs=seqs,
                          n_cast=len(cast_srcs)),
        grid=(steps,), in_specs=in_specs,
        out_specs=[tok, s_spec, v_spec] + cast_out_specs,
        out_shape=[jax.ShapeDtypeStruct((tokens, d), F32),
                   jax.ShapeDtypeStruct(state_ret.shape, F32),
                   jax.ShapeDtypeStruct((steps, seqs, RET_V), F32)] + cast_out_shapes,
        input_output_aliases=aliases,
        compiler_params=pltpu.CompilerParams(
            dimension_semantics=("arbitrary",), vmem_limit_bytes=vmem),
        name="prompt_ffn_sample_state",
    )(*args)
    return outs[0], outs[1], outs[2].reshape(n, RET_V), tuple(outs[3:])


PRE_SECTION = 2 * GM_WIDTH
assert OFF_Q == PRE_SECTION and OFF_VR == 2 * PRE_SECTION and OFF_GR == 3 * PRE_SECTION
assert OFF_GA == 4 * PRE_SECTION and IN_WIDTH == 5 * PRE_SECTION


def _sample_pre_kernel(x_ref, cos_ref, sin_ref, ng_ref, bg_ref, lng_ref, lnb_ref, wsd_ref,
                       bsd_ref, win_hbm, v_ref, a_ref, q_ref, k_ref, vr_ref, sg_ref, gate_ref,
                       w_scr, sems):
    copies = [pltpu.make_async_copy(win_hbm.at[:, pl.ds(s * PRE_SECTION, PRE_SECTION)],
                                    w_scr.at[s], sems.at[s])
              for s in range(IN_WIDTH // PRE_SECTION)]
    for cp in copies:
        cp.start()
    xn = _rms(x_ref[...], ng_ref[...]).astype(BF16)

    copies[0].wait()
    u = jax.nn.gelu(_dot(xn, w_scr[0, :, :GM_WIDTH]))
    v = jax.nn.gelu(_dot(xn, w_scr[0, :, GM_WIDTH:]))
    v = _layer_norm(v, lng_ref[...], lnb_ref[...])
    v_ref[...] = v
    a_ref[...] = (u * (wsd_ref[...] * v + bsd_ref[...])).astype(BF16)

    copies[1].wait()
    cos = cos_ref[...]
    sin = sin_ref[...]
    for h in range(RET_HEADS):
        qs = slice(h * RET_DK, (h + 1) * RET_DK)
        q = _dot(xn, w_scr[1, :, h * RET_DK:(h + 1) * RET_DK])
        k = _dot(xn, w_scr[1, :, RET_QK + h * RET_DK:RET_QK + (h + 1) * RET_DK])
        q_ref[:, qs] = _rotate(q, cos, sin)
        k_ref[:, qs] = _rotate(k, cos, sin) * (RET_DK ** -0.5)
    copies[2].wait()
    vr_ref[...] = _dot(xn, w_scr[2])
    copies[3].wait()
    sg_ref[...] = jax.nn.silu(_dot(xn, w_scr[3]))
    copies[4].wait()
    gate_ref[...] = jax.nn.sigmoid(_dot(xn, w_scr[4]) + bg_ref[...])


def _sample_pre_call(x, cos, sin, ng, win, bg, lng, lnb, wsd, bsd):
    n = x.shape[0]
    out_shape = [jax.ShapeDtypeStruct((n, GM_WIDTH), F32),
                 jax.ShapeDtypeStruct((n, GM_WIDTH), BF16),
                 jax.ShapeDtypeStruct((n, RET_QK), F32),
                 jax.ShapeDtypeStruct((n, RET_QK), F32),
                 jax.ShapeDtypeStruct((n, RET_V), F32),
                 jax.ShapeDtypeStruct((n, RET_V), F32),
                 jax.ShapeDtypeStruct((n, 2 * D_MODEL), F32)]
    args = [x, cos, sin, ng, bg, lng, lnb, wsd, bsd]
    n_sec = IN_WIDTH // PRE_SECTION
    w_scr = ((n_sec, D_MODEL, PRE_SECTION), BF16)
    vmem = _vmem_limit(whole=[(a.shape, a.dtype) for a in args + out_shape] + [w_scr],
                       streamed=[], temp_tiles=8)
    return pl.pallas_call(
        _sample_pre_kernel, grid=(1,),
        in_specs=[_full(a.shape) for a in args] + [pl.BlockSpec(memory_space=pl.ANY)],
        out_specs=[_full(s.shape) for s in out_shape], out_shape=out_shape,
        scratch_shapes=[pltpu.VMEM(*w_scr), pltpu.SemaphoreType.DMA((n_sec,))],
        compiler_params=pltpu.CompilerParams(
            dimension_semantics=("arbitrary",), vmem_limit_bytes=vmem),
        name="sample_pre",
    )(*args, win)


def _sample_post_kernel(x_ref, o_ref, sg_ref, gate_ref, a_ref, nfg_ref, gf_ref,
                        wpg_hbm, wpr_hbm, wo_hbm, wup_hbm, wdn_hbm, y_ref,
                        wpg_v, wpr_v, wo_v, wup_v, wdn_v, sems, *, final):
    moves = [(wpg_hbm, wpg_v), (wpr_hbm, wpr_v), (wo_hbm, wo_v)]
    for j in range(D_FF // FF_SPLIT):
        blk = pl.ds(j * FF_SPLIT, FF_SPLIT)
        moves.append((wup_hbm.at[:, blk], wup_v.at[:, blk]))
        moves.append((wdn_hbm.at[blk, :], wdn_v.at[blk, :]))
    copies = [pltpu.make_async_copy(src, dst, sems.at[i]) for i, (src, dst) in enumerate(moves)]
    for cp in copies:
        cp.start()

    rs = []
    for h in range(RET_HEADS):
        hs = slice(h * RET_DV, (h + 1) * RET_DV)
        o = o_ref[:, hs]
        on = o * lax.rsqrt(jnp.mean(o * o, axis=-1, keepdims=True) + EPS)
        rs.append((sg_ref[:, hs] * on).astype(BF16))
    r = jnp.concatenate(rs, axis=-1)
    copies[0].wait()
    hm = gate_ref[:, :D_MODEL] * _dot(a_ref[...], wpg_v[...])
    copies[1].wait()
    hm = hm + gate_ref[:, D_MODEL:] * _dot(r, wpr_v[...])
    copies[2].wait()
    x = x_ref[...] + _dot(hm.astype(BF16), wo_v[...])

    def wait_block(j):
        copies[3 + 2 * j].wait()
        copies[4 + 2 * j].wait()

    y = _ffn_body(x, _rms(x, nfg_ref[...]).astype(BF16), wup_v, wdn_v, before_block=wait_block)
    if final:
        y = _rms(y, gf_ref[...])
    y_ref[...] = y


def _sample_post_call(x, o, sg, gate, a, wpg, wpr, wo, nfg, wup, wdn, gf, final):
    args = [x, o, sg, gate, a, nfg, gf]
    weights = [wpg, wpr, wo, wup, wdn]
    n_copies = 3 + 2 * (D_FF // FF_SPLIT)
    vmem = _vmem_limit(whole=[(v.shape, v.dtype) for v in args + weights + [x]], streamed=[],
                       temp_tiles=8)
    return pl.pallas_call(
        functools.partial(_sample_post_kernel, final=final),
        grid=(1,),
        in_specs=[_full(v.shape) for v in args] + [pl.BlockSpec(memory_space=pl.ANY)] * len(weights),
        out_specs=_full(x.shape),
        out_shape=jax.ShapeDtypeStruct(x.shape, F32),
        scratch_shapes=[pltpu.VMEM(w.shape, w.dtype) for w in weights]
        + [pltpu.SemaphoreType.DMA((n_copies,))],
        compiler_params=pltpu.CompilerParams(
            dimension_semantics=("arbitrary",), vmem_limit_bytes=vmem),
        name="sample_post",
    )(*args, *weights)


def kernel(x_prompt, x_sample, state_ret, norm_mix_g, w_in, b_gate, gm_ln_g, gm_ln_b, gm_w_s, gm_b_s,
           w_proj_gm, w_proj_ret, w_out, norm_ffn_g, w_up, w_down, norm_final_g):
    b, l, d = x_prompt.shape
    n = x_sample.shape[0]
    assert l % MIX_TILE == 0 and MIX_TILE % RET_CHUNK == 0 and x_sample.shape[1] == 1

    intra, qdec, kdec, chunk_decay = _retention_constants(RET_CHUNK)
    cos_p, sin_p = _rope_tables(0, l)
    cos_s, sin_s = _rope_tables(PAST_LEN, 1)
    row = lambda a: a.reshape(1, -1)
    gf = row(norm_final_g)

    mixer_f32 = (w_in, w_proj_gm, w_proj_ret, w_out)
    ffn_f32 = (w_up, w_down)
    win, wpg, wpr, wo = (w[0].astype(BF16) for w in mixer_f32)

    xp = x_prompt
    xs = x_sample.reshape(n, d)
    v_rows = []
    prompt_states = None
    sample_states = None
    for layer in range(DEPTH):
        ng = row(norm_mix_g[layer])
        bg = row(b_gate[layer])
        lng = row(gm_ln_g[layer])
        lnb = row(gm_ln_b[layer])
        nfg = row(norm_ffn_g[layer])
        final = layer == DEPTH - 1

        wsd = row(jnp.repeat(gm_w_s[layer, :, 0, 0], GM_GROUP_DIM))
        bsd = row(jnp.repeat(gm_b_s[layer, :, 0], GM_GROUP_DIM))
        v_s, a_s, q_s, k_s, vr_s, sg_s, gate_s = _sample_pre_call(
            xs, cos_s, sin_s, ng, win, bg, lng, lnb, wsd, bsd)
        v_rows.append(v_s)

        xp, prompt_states, (wup, wdn) = _mixer_call(
            layer, xp, cos_p, sin_p, ng, win, bg, lng, lnb, gm_w_s[layer], gm_b_s[layer].T,
            intra, qdec, kdec, wpg, wpr, wo, chunk_decay, prompt_states, ffn_f32)
        xp, sample_states, o_s, next_mixer = _ffn_state_call(
            layer, xp.reshape(b * l, d), nfg, wup, wdn, gf, final, state_ret, q_s, k_s, vr_s,
            sample_states, () if final else mixer_f32)
        xp = xp.reshape(b, l, d)
        xs = _sample_post_call(xs, o_s, sg_s, gate_s, a_s, wpg, wpr, wo, nfg, wup, wdn, gf, final)
        if not final:
            win, wpg, wpr, wo = next_mixer

    return (xp, xs.reshape(n, 1, d), prompt_states, sample_states,
            jnp.stack(v_rows).reshape(DEPTH, n, 1, GM_WIDTH))
```

```python
import functools

import numpy as np
import jax
import jax.numpy as jnp
from jax import lax
from jax.experimental import pallas as pl
from jax.experimental.pallas import tpu as pltpu

D_MODEL = 1024
DEPTH = 4
PAST_LEN = 16384
GM_WIDTH = D_MODEL
GM_GROUPS = 4
GM_GROUP_DIM = GM_WIDTH // GM_GROUPS
GM_CHUNK = 128
RET_HEADS = 4
RET_DK = D_MODEL // RET_HEADS
RET_DV = 2 * RET_DK
RET_QK = RET_HEADS * RET_DK
RET_V = RET_HEADS * RET_DV
ROPE_BASE = 10000.0
D_FF = 4 * D_MODEL
EPS = 1e-6

OFF_U = 0
OFF_V = GM_WIDTH
OFF_Q = 2 * GM_WIDTH
OFF_K = OFF_Q + RET_QK
OFF_VR = OFF_K + RET_QK
OFF_GR = OFF_VR + RET_V
OFF_GA = OFF_GR + RET_V
OFF_GB = OFF_GA + D_MODEL
IN_WIDTH = OFF_GB + D_MODEL

V7X_VMEM_BYTES = 64 * 1024 * 1024
V7X_LANES = 128
V7X_SUBLANES = 8
VMEM_REQUEST_FLOOR = V7X_VMEM_BYTES * 7 // 8

MIX_TILE = 512
RET_CHUNK = 256
FFN_TILE = 256
FF_SPLIT = 1024

BF16 = jnp.bfloat16
F32 = jnp.float32


def _dot(a, b):
    return jnp.dot(a, b, preferred_element_type=F32)


def _dot_nt(a, b):
    return lax.dot_general(a, b, (((1,), (1,)), ((), ())), preferred_element_type=F32)


def _dot_tn(a, b):
    return lax.dot_general(a, b, (((0,), (0,)), ((), ())), preferred_element_type=F32)


def _rms(x, g):
    return x * lax.rsqrt(jnp.mean(x * x, axis=-1, keepdims=True) + EPS) * g


def _layer_norm(x, g, b):
    mu = jnp.mean(x, axis=-1, keepdims=True)
    xc = x - mu
    return xc * lax.rsqrt(jnp.mean(xc * xc, axis=-1, keepdims=True) + EPS) * g + b


def _rotate(x, cos, sin):
    half = RET_DK // 2
    x1 = x[:, :half]
    x2 = x[:, half:]
    return jnp.concatenate([x1 * cos - x2 * sin, x1 * sin + x2 * cos], axis=-1)


def _head_gammas():
    return tuple(1.0 - 2.0 ** (-5.0 - h) for h in range(RET_HEADS))


def _full(shape):
    n = len(shape)
    return pl.BlockSpec(shape, lambda *_: (0,) * n)


def _window_bytes(shape, dtype):
    itemsize = jnp.dtype(dtype).itemsize
    dims = [d for d in shape if d is not None]
    dims[-1] = -(-dims[-1] // V7X_LANES) * V7X_LANES
    if len(dims) > 1:
        sub = V7X_SUBLANES * 4 // itemsize
        dims[-2] = -(-dims[-2] // sub) * sub
    return int(np.prod(dims)) * itemsize


def _vmem_limit(whole, streamed, temp_tiles):
    need = (sum(_window_bytes(s, d) for s, d in whole)
            + 2 * sum(_window_bytes(s, d) for s, d in streamed)
            + temp_tiles * _window_bytes((RET_CHUNK, D_MODEL), F32))
    assert need <= V7X_VMEM_BYTES, need
    return max(need, VMEM_REQUEST_FLOOR)


def _cast_specs(stacked, layer, steps, step_of):
    _, rows, cols = stacked.shape
    r = rows // steps
    assert r * steps == rows
    src = pl.BlockSpec((None, r, cols), lambda *idx: (layer, step_of(*idx), 0))
    dst = pl.BlockSpec((r, cols), lambda *idx: (step_of(*idx), 0))
    return src, dst


def _cast_rows(cast_in, cast_out):
    for src, dst in zip(cast_in, cast_out):
        dst[...] = src[...].astype(BF16)


def _mixer_kernel(*refs, chunk_decay, n_cast):
    (x_ref, cos_ref, sin_ref, ng_ref, win_ref, bg_ref, lng_ref, lnb_ref, ws_ref, bst_ref,
     intra_ref, qdec_ref, kdec_ref, wpg_ref, wpr_ref, wo_ref) = refs[:16]
    cast_in = refs[16:16 + n_cast]
    y_ref, s_ref = refs[-4 - n_cast:-2 - n_cast]
    cast_out = refs[-2 - n_cast:-2]
    a_scr, r_scr = refs[-2:]
    c = RET_CHUNK
    n_chunks = x_ref.shape[0] // c

    @pl.when(pl.program_id(1) == 0)
    def _():
        s_ref[...] = jnp.zeros(s_ref.shape, F32)

    row = lax.broadcasted_iota(jnp.int32, (GM_CHUNK, GM_CHUNK), 0)
    col = lax.broadcasted_iota(jnp.int32, (GM_CHUNK, GM_CHUNK), 1)

    def chunk_body(i):
        rows = pl.ds(i * c, c)
        x = x_ref[rows, :]
        xn = _rms(x, ng_ref[...]).astype(BF16)

        u = jax.nn.gelu(_dot(xn, win_ref[:, OFF_U:OFF_U + GM_WIDTH]))
        v = jax.nn.gelu(_dot(xn, win_ref[:, OFF_V:OFF_V + GM_WIDTH]))
        ga = jax.nn.sigmoid(_dot(xn, win_ref[:, OFF_GA:OFF_GA + D_MODEL]) + bg_ref[:, :D_MODEL])
        gb = jax.nn.sigmoid(_dot(xn, win_ref[:, OFF_GB:OFF_GB + D_MODEL]) + bg_ref[:, D_MODEL:])

        vb = _layer_norm(v, lng_ref[...], lnb_ref[...]).astype(BF16)
        for g in range(GM_GROUPS):
            w_tril = jnp.where(row >= col, ws_ref[g], 0.0).astype(BF16)
            b_col = bst_ref[:, g:g + 1]
            cs = slice(g * GM_GROUP_DIM, (g + 1) * GM_GROUP_DIM)
            for k in range(c // GM_CHUNK):
                rs = slice(k * GM_CHUNK, (k + 1) * GM_CHUNK)
                mixed = _dot(w_tril, vb[rs, cs]) + b_col
                a_scr[rs, cs] = (u[rs, cs] * mixed).astype(BF16)

        cos = cos_ref[rows, :]
        sin = sin_ref[rows, :]
        for h in range(RET_HEADS):
            q = _dot(xn, win_ref[:, OFF_Q + h * RET_DK:OFF_Q + (h + 1) * RET_DK])
            k = _dot(xn, win_ref[:, OFF_K + h * RET_DK:OFF_K + (h + 1) * RET_DK])
            vr = _dot(xn, win_ref[:, OFF_VR + h * RET_DV:OFF_VR + (h + 1) * RET_DV]).astype(BF16)
            gr = _dot(xn, win_ref[:, OFF_GR + h * RET_DV:OFF_GR + (h + 1) * RET_DV])
            qr = _rotate(q, cos, sin).astype(BF16)
            kr = _rotate(k, cos, sin) * (RET_DK ** -0.5)
            s_old = s_ref[h]
            scores = _dot_nt(qr, kr.astype(BF16)) * intra_ref[h]
            cross = _dot(qr, s_old.astype(BF16)) * qdec_ref[h]
            s_ref[h] = s_old * chunk_decay[h] + _dot_tn((kr * kdec_ref[h]).astype(BF16), vr)
            o = _dot(scores.astype(BF16), vr) + cross
            on = o * lax.rsqrt(jnp.mean(o * o, axis=-1, keepdims=True) + EPS)
            r_scr[:, h * RET_DV:(h + 1) * RET_DV] = (jax.nn.silu(gr) * on).astype(BF16)

        hm = ga * _dot(a_scr[...], wpg_ref[...]) + gb * _dot(r_scr[...], wpr_ref[...])
        y_ref[rows, :] = x + _dot(hm.astype(BF16), wo_ref[...])

        for src, dst in zip(cast_in, cast_out):
            r = src.shape[0] // n_chunks
            part = pl.ds(i * r, r)
            dst[part, :] = src[part, :].astype(BF16)

    for i in range(n_chunks):
        chunk_body(i)


def _retention_constants(c):
    log_g = np.log(np.array(_head_gammas(), np.float64))
    idx = np.arange(c, dtype=np.float64)
    diff = idx[:, None] - idx[None, :]
    intra = np.where(diff[None] >= 0, np.exp(diff[None] * log_g[:, None, None]), 0.0)
    qdec = np.exp((idx[None, :, None] + 1.0) * log_g[:, None, None])
    kdec = np.exp((c - 1.0 - idx[None, :, None]) * log_g[:, None, None])
    chunk_decay = tuple(float(v) for v in np.exp(c * log_g).astype(np.float32))
    return (jnp.asarray(intra, F32), jnp.asarray(qdec, F32), jnp.asarray(kdec, F32), chunk_decay)


def _rope_tables(pos0, length):
    inv_freq = 1.0 / (ROPE_BASE ** jnp.linspace(0.0, 1.0, RET_DK // 2, dtype=F32))
    pos = (jnp.arange(length, dtype=jnp.int32) + pos0).astype(F32)
    ang = pos[:, None] * inv_freq[None, :]
    return jnp.cos(ang), jnp.sin(ang)


def _mixer_call(layer, x, cos, sin, ng, win, bg, lng, lnb, ws, bst, intra, qdec, kdec, wpg, wpr,
                wo, chunk_decay, prev_states, cast_srcs):
    b, l, d = x.shape
    t = MIX_TILE
    per_row = l // t
    steps = b * per_row
    tok = pl.BlockSpec((None, t, d), lambda i, j: (i, j, 0))
    rope = pl.BlockSpec((t, RET_DK // 2), lambda i, j: (j, 0))
    in_specs = [tok, rope, rope, _full(ng.shape), _full(win.shape), _full(bg.shape),
                _full(lng.shape), _full(lnb.shape), _full(ws.shape), _full(bst.shape),
                _full(intra.shape), _full(qdec.shape), _full(kdec.shape),
                _full(wpg.shape), _full(wpr.shape), _full(wo.shape)]
    args = [x, cos, sin, ng, win, bg, lng, lnb, ws, bst, intra, qdec, kdec, wpg, wpr, wo]
    cast_out_specs = []
    cast_out_shapes = []
    for w in cast_srcs:
        src, dst = _cast_specs(w, layer, steps, lambda i, j: i * per_row + j)
        in_specs.append(src)
        args.append(w)
        cast_out_specs.append(dst)
        cast_out_shapes.append(jax.ShapeDtypeStruct(w.shape[1:], BF16))
    aliases = {}
    if prev_states is not None:
        in_specs.append(pl.BlockSpec(memory_space=pl.ANY))
        args.append(prev_states)
        aliases = {len(args) - 1: 1}
    state_spec = pl.BlockSpec((None, None, RET_HEADS, RET_DK, RET_DV),
                              lambda i, j: (layer, i, 0, 0, 0))
    scratch = [((RET_CHUNK, GM_WIDTH), BF16), ((RET_CHUNK, RET_V), BF16)]
    vmem = _vmem_limit(
        whole=[(a.shape, a.dtype) for a in args[3:16]] + scratch,
        streamed=[((t, d), F32)] * 2 + [((t, RET_DK // 2), F32)] * 2
        + [((RET_HEADS, RET_DK, RET_DV), F32)]
        + [(s.block_shape, dt) for s, dt in zip(in_specs[16:16 + len(cast_srcs)],
                                                 [F32] * len(cast_srcs))]
        + [(s.block_shape, BF16) for s in cast_out_specs],
        temp_tiles=10)
    outs = pl.pallas_call(
        functools.partial(_mixer_kernel, chunk_decay=chunk_decay, n_cast=len(cast_srcs)),
        grid=(b, per_row), in_specs=in_specs,
        out_specs=[tok, state_spec] + cast_out_specs,
        out_shape=[jax.ShapeDtypeStruct((b, l, d), F32),
                   jax.ShapeDtypeStruct((DEPTH, b, RET_HEADS, RET_DK, RET_DV), F32)]
        + cast_out_shapes,
        scratch_shapes=[pltpu.VMEM(s, dt) for s, dt in scratch],
        input_output_aliases=aliases,
        compiler_params=pltpu.CompilerParams(
            dimension_semantics=("arbitrary", "arbitrary"), vmem_limit_bytes=vmem),
        name="prompt_mixer",
    )(*args)
    return outs[0], outs[1], tuple(outs[2:])


def _ffn_body(x, hn, wup_ref, wdn_ref, anchor=None, before_block=None):
    acc = x
    for j in range(D_FF // FF_SPLIT):
        cols = slice(j * FF_SPLIT, (j + 1) * FF_SPLIT)
        if before_block is not None:
            before_block(j)
        up = _dot(hn, wup_ref[:, cols])
        if anchor is not None:
            up = up + anchor[:, cols]
        act = jnp.square(jnp.maximum(up, 0.0)).astype(BF16)
        acc = acc + _dot(act, wdn_ref[cols, :])
    return acc


def _zero_of(v):
    bits = pltpu.bitcast(v, jnp.int32)
    bits = lax.shift_right_logical(lax.shift_right_logical(bits, 16), 16)
    return bits.astype(F32)


def _ffn_state_kernel(*refs, final, gammas, seqs, n_cast):
    x_ref, g_ref, wup_ref, wdn_ref, gf_ref, s_ref, qk_ref, v_ref = refs[:8]
    cast_in = refs[8:8 + n_cast]
    y_ref, snew_ref, o_ref = refs[-3 - n_cast:len(refs) - n_cast]
    cast_out = refs[len(refs) - n_cast:]

    _cast_rows(cast_in, cast_out)

    zeros = []
    for j in range(seqs):
        for h in range(RET_HEADS):
            q_col = qk_ref[:, j * RET_HEADS + h:j * RET_HEADS + h + 1]
            k_col = qk_ref[:, (seqs + j) * RET_HEADS + h:(seqs + j) * RET_HEADS + h + 1]
            v_row = v_ref[j:j + 1, h * RET_DV:(h + 1) * RET_DV]
            snew_ref[j, h] = s_ref[j, h] * gammas[h] + k_col * v_row
            o = jnp.sum(q_col * snew_ref[j, h], axis=0, keepdims=True)
            o_ref[j:j + 1, h * RET_DV:(h + 1) * RET_DV] = o
            zeros.append(_zero_of(o))

    anchor = jnp.concatenate(zeros, axis=-1)
    assert anchor.shape == (1, D_FF)
    x = x_ref[...]
    y = _ffn_body(x, _rms(x, g_ref[...]).astype(BF16), wup_ref, wdn_ref, anchor)
    if final:
        y = _rms(y, gf_ref[...])
    y_ref[...] = y


def _ffn_state_call(layer, x2d, g, wup, wdn, gf, final, state_ret, q_s, k_s, v_s, prev_out,
                    cast_srcs):
    tokens, d = x2d.shape
    n = q_s.shape[0]
    t = FFN_TILE
    steps = tokens // t
    seqs = n // steps
    assert steps * t == tokens and seqs * steps == n

    q4 = q_s.reshape(steps, seqs, RET_HEADS, RET_DK)
    k4 = k_s.reshape(steps, seqs, RET_HEADS, RET_DK)
    qk = jnp.stack([q4, k4], axis=0).transpose(1, 4, 0, 2, 3).reshape(
        steps, RET_DK, 2 * seqs * RET_HEADS)
    v3 = v_s.reshape(steps, seqs, RET_V)

    tok = pl.BlockSpec((t, d), lambda i: (i, 0))
    s_spec = pl.BlockSpec((None, seqs, RET_HEADS, RET_DK, RET_DV), lambda i: (layer, i, 0, 0, 0))
    qk_spec = pl.BlockSpec((None, RET_DK, 2 * seqs * RET_HEADS), lambda i: (i, 0, 0))
    v_spec = pl.BlockSpec((None, seqs, RET_V), lambda i: (i, 0, 0))
    in_specs = [tok, _full(g.shape), _full(wup.shape), _full(wdn.shape), _full(gf.shape),
                s_spec, qk_spec, v_spec]
    args = [x2d, g, wup, wdn, gf, state_ret, qk, v3]
    cast_out_specs = []
    cast_out_shapes = []
    for w in cast_srcs:
        src, dst = _cast_specs(w, layer + 1, steps, lambda i: i)
        in_specs.append(src)
        args.append(w)
        cast_out_specs.append(dst)
        cast_out_shapes.append(jax.ShapeDtypeStruct(w.shape[1:], BF16))
    aliases = {}
    if prev_out is not None:
        in_specs.append(pl.BlockSpec(memory_space=pl.ANY))
        args.append(prev_out)
        aliases = {len(args) - 1: 1}
    vmem = _vmem_limit(
        whole=[(a.shape, a.dtype) for a in (g, wup, wdn, gf)],
        streamed=[(tok.block_shape, F32), (s_spec.block_shape, F32)] * 2
        + [(qk_spec.block_shape, F32)] + [(v_spec.block_shape, F32)] * 2
        + [(s.block_shape, F32) for s in in_specs[8:8 + len(cast_srcs)]]
        + [(s.block_shape, BF16) for s in cast_out_specs],
        temp_tiles=6)
    outs = pl.pallas_call(
        functools.partial(_ffn_state_kernel, final=final, gammas=_head_gammas(), seqs=seqs,
                          n_cast=len(cast_srcs)),
        grid=(steps,), in_specs=in_specs,
        out_specs=[tok, s_spec, v_spec] + cast_out_specs,
        out_shape=[jax.ShapeDtypeStruct((tokens, d), F32),
                   jax.ShapeDtypeStruct(state_ret.shape, F32),
                   jax.ShapeDtypeStruct((steps, seqs, RET_V), F32)] + cast_out_shapes,
        input_output_aliases=aliases,
        compiler_params=pltpu.CompilerParams(
            dimension_semantics=("arbitrary",), vmem_limit_bytes=vmem),
        name="prompt_ffn_sample_state",
    )(*args)
    return outs[0], outs[1], outs[2].reshape(n, RET_V), tuple(outs[3:])


PRE_SECTION = 2 * GM_WIDTH
assert OFF_Q == PRE_SECTION and OFF_VR == 2 * PRE_SECTION and OFF_GR == 3 * PRE_SECTION
assert OFF_GA == 4 * PRE_SECTION and IN_WIDTH == 5 * PRE_SECTION


def _sample_pre_kernel(x_ref, cos_ref, sin_ref, ng_ref, bg_ref, lng_ref, lnb_ref, wsd_ref,
                       bsd_ref, win_hbm, v_ref, a_ref, q_ref, k_ref, vr_ref, sg_ref, gate_ref,
                       w_scr, sems):
    copies = [pltpu.make_async_copy(win_hbm.at[:, pl.ds(s * PRE_SECTION, PRE_SECTION)],
                                    w_scr.at[s], sems.at[s])
              for s in range(IN_WIDTH // PRE_SECTION)]
    for cp in copies:
        cp.start()
    xn = _rms(x_ref[...], ng_ref[...]).astype(BF16)

    copies[0].wait()
    u = jax.nn.gelu(_dot(xn, w_scr[0, :, :GM_WIDTH]))
    v = jax.nn.gelu(_dot(xn, w_scr[0, :, GM_WIDTH:]))
    v = _layer_norm(v, lng_ref[...], lnb_ref[...])
    v_ref[...] = v
    a_ref[...] = (u * (wsd_ref[...] * v + bsd_ref[...])).astype(BF16)

    copies[1].wait()
    cos = cos_ref[...]
    sin = sin_ref[...]
    for h in range(RET_HEADS):
        qs = slice(h * RET_DK, (h + 1) * RET_DK)
        q = _dot(xn, w_scr[1, :, h * RET_DK:(h + 1) * RET_DK])
        k = _dot(xn, w_scr[1, :, RET_QK + h * RET_DK:RET_QK + (h + 1) * RET_DK])
        q_ref[:, qs] = _rotate(q, cos, sin)
        k_ref[:, qs] = _rotate(k, cos, sin) * (RET_DK ** -0.5)
    copies[2].wait()
    vr_ref[...] = _dot(xn, w_scr[2])
    copies[3].wait()
    sg_ref[...] = jax.nn.silu(_dot(xn, w_scr[3]))
    copies[4].wait()
    gate_ref[...] = jax.nn.sigmoid(_dot(xn, w_scr[4]) + bg_ref[...])


def _sample_pre_call(x, cos, sin, ng, win, bg, lng, lnb, wsd, bsd):
    n = x.shape[0]
    out_shape = [jax.ShapeDtypeStruct((n, GM_WIDTH), F32),
                 jax.ShapeDtypeStruct((n, GM_WIDTH), BF16),
                 jax.ShapeDtypeStruct((n, RET_QK), F32),
                 jax.ShapeDtypeStruct((n, RET_QK), F32),
                 jax.ShapeDtypeStruct((n, RET_V), F32),
                 jax.ShapeDtypeStruct((n, RET_V), F32),
                 jax.ShapeDtypeStruct((n, 2 * D_MODEL), F32)]
    args = [x, cos, sin, ng, bg, lng, lnb, wsd, bsd]
    n_sec = IN_WIDTH // PRE_SECTION
    w_scr = ((n_sec, D_MODEL, PRE_SECTION), BF16)
    vmem = _vmem_limit(whole=[(a.shape, a.dtype) for a in args + out_shape] + [w_scr],
                       streamed=[], temp_tiles=8)
    return pl.pallas_call(
        _sample_pre_kernel, grid=(1,),
        in_specs=[_full(a.shape) for a in args] + [pl.BlockSpec(memory_space=pl.ANY)],
        out_specs=[_full(s.shape) for s in out_shape], out_shape=out_shape,
        scratch_shapes=[pltpu.VMEM(*w_scr), pltpu.SemaphoreType.DMA((n_sec,))],
        compiler_params=pltpu.CompilerParams(
            dimension_semantics=("arbitrary",), vmem_limit_bytes=vmem),
        name="sample_pre",
    )(*args, win)


def _sample_post_kernel(x_ref, o_ref, sg_ref, gate_ref, a_ref, nfg_ref, gf_ref,
                        wpg_hbm, wpr_hbm, wo_hbm, wup_hbm, wdn_hbm, y_ref,
                        wpg_v, wpr_v, wo_v, wup_v, wdn_v, sems, *, final):
    moves = [(wpg_hbm, wpg_v), (wpr_hbm, wpr_v), (wo_hbm, wo_v)]
    for j in range(D_FF // FF_SPLIT):
        blk = pl.ds(j * FF_SPLIT, FF_SPLIT)
        moves.append((wup_hbm.at[:, blk], wup_v.at[:, blk]))
        moves.append((wdn_hbm.at[blk, :], wdn_v.at[blk, :]))
    copies = [pltpu.make_async_copy(src, dst, sems.at[i]) for i, (src, dst) in enumerate(moves)]
    for cp in copies:
        cp.start()

    rs = []
    for h in range(RET_HEADS):
        hs = slice(h * RET_DV, (h + 1) * RET_DV)
        o = o_ref[:, hs]
        on = o * lax.rsqrt(jnp.mean(o * o, axis=-1, keepdims=True) + EPS)
        rs.append((sg_ref[:, hs] * on).astype(BF16))
    r = jnp.concatenate(rs, axis=-1)
    copies[0].wait()
    hm = gate_ref[:, :D_MODEL] * _dot(a_ref[...], wpg_v[...])
    copies[1].wait()
    hm = hm + gate_ref[:, D_MODEL:] * _dot(r, wpr_v[...])
    copies[2].wait()
    x = x_ref[...] + _dot(hm.astype(BF16), wo_v[...])

    def wait_block(j):
        copies[3 + 2 * j].wait()
        copies[4 + 2 * j].wait()

    y = _ffn_body(x, _rms(x, nfg_ref[...]).astype(BF16), wup_v, wdn_v, before_block=wait_block)
    if final:
        y = _rms(y, gf_ref[...])
    y_ref[...] = y


def _sample_post_call(x, o, sg, gate, a, wpg, wpr, wo, nfg, wup, wdn, gf, final):
    args = [x, o, sg, gate, a, nfg, gf]
    weights = [wpg, wpr, wo, wup, wdn]
    n_copies = 3 + 2 * (D_FF // FF_SPLIT)
    vmem = _vmem_limit(whole=[(v.shape, v.dtype) for v in args + weights + [x]], streamed=[],
                       temp_tiles=8)
    return pl.pallas_call(
        functools.partial(_sample_post_kernel, final=final),
        grid=(1,),
        in_specs=[_full(v.shape) for v in args] + [pl.BlockSpec(memory_space=pl.ANY)] * len(weights),
        out_specs=_full(x.shape),
        out_shape=jax.ShapeDtypeStruct(x.shape, F32),
        scratch_shapes=[pltpu.VMEM(w.shape, w.dtype) for w in weights]
        + [pltpu.SemaphoreType.DMA((n_copies,))],
        compiler_params=pltpu.CompilerParams(
            dimension_semantics=("arbitrary",), vmem_limit_bytes=vmem),
        name="sample_post",
    )(*args, *weights)


def kernel(x_prompt, x_sample, state_ret, norm_mix_g, w_in, b_gate, gm_ln_g, gm_ln_b, gm_w_s, gm_b_s,
           w_proj_gm, w_proj_ret, w_out, norm_ffn_g, w_up, w_down, norm_final_g):
    b, l, d = x_prompt.shape
    n = x_sample.shape[0]
    assert l % MIX_TILE == 0 and MIX_TILE % RET_CHUNK == 0 and x_sample.shape[1] == 1

    intra, qdec, kdec, chunk_decay = _retention_constants(RET_CHUNK)
    cos_p, sin_p = _rope_tables(0, l)
    cos_s, sin_s = _rope_tables(PAST_LEN, 1)
    row = lambda a: a.reshape(1, -1)
    gf = row(norm_final_g)

    mixer_f32 = (w_in, w_proj_gm, w_proj_ret, w_out)
    ffn_f32 = (w_up, w_down)
    win, wpg, wpr, wo = (w[0].astype(BF16) for w in mixer_f32)

    xp = x_prompt
    xs = x_sample.reshape(n, d)
    v_rows = []
    prompt_states = None
    sample_states = None
    for layer in range(DEPTH):
        ng = row(norm_mix_g[layer])
        bg = row(b_gate[layer])
        lng = row(gm_ln_g[layer])
        lnb = row(gm_ln_b[layer])
        nfg = row(norm_ffn_g[layer])
        final = layer == DEPTH - 1

        wsd = row(jnp.repeat(gm_w_s[layer, :, 0, 0], GM_GROUP_DIM))
        bsd = row(jnp.repeat(gm_b_s[layer, :, 0], GM_GROUP_DIM))
        v_s, a_s, q_s, k_s, vr_s, sg_s, gate_s = _sample_pre_call(
            xs, cos_s, sin_s, ng, win, bg, lng, lnb, wsd, bsd)
        v_rows.append(v_s)

        xp, prompt_states, (wup, wdn) = _mixer_call(
            layer, xp, cos_p, sin_p, ng, win, bg, lng, lnb, gm_w_s[layer], gm_b_s[layer].T,
            intra, qdec, kdec, wpg, wpr, wo, chunk_decay, prompt_states, ffn_f32)
        xp, sample_states, o_s, next_mixer = _ffn_state_call(
            layer, xp.reshape(b * l, d), nfg, wup, wdn, gf, final, state_ret, q_s, k_s, vr_s,
            sample_states, () if final else mixer_f32)
        xp = xp.reshape(b, l, d)
        xs = _sample_post_call(xs, o_s, sg_s, gate_s, a_s, wpg, wpr, wo, nfg, wup, wdn, gf, final)
        if not final:
            win, wpg, wpr, wo = next_mixer

    return (xp, xs.reshape(n, 1, d), prompt_states, sample_states,
            jnp.stack(v_rows).reshape(DEPTH, n, 1, GM_WIDTH))
```

```python
import functools

import numpy as np
import jax
import jax.numpy as jnp
from jax import lax
from jax.experimental import pallas as pl
from jax.experimental.pallas import tpu as pltpu

D_MODEL = 1024
DEPTH = 4
PAST_LEN = 16384
GM_WIDTH = D_MODEL
GM_GROUPS = 4
GM_GROUP_DIM = GM_WIDTH // GM_GROUPS
GM_CHUNK = 128
RET_HEADS = 4
RET_DK = D_MODEL // RET_HEADS
RET_DV = 2 * RET_DK
RET_QK = RET_HEADS * RET_DK
RET_V = RET_HEADS * RET_DV
ROPE_BASE = 10000.0
D_FF = 4 * D_MODEL
EPS = 1e-6

OFF_U = 0
OFF_V = GM_WIDTH
OFF_Q = 2 * GM_WIDTH
OFF_K = OFF_Q + RET_QK
OFF_VR = OFF_K + RET_QK
OFF_GR = OFF_VR + RET_V
OFF_GA = OFF_GR + RET_V
OFF_GB = OFF_GA + D_MODEL
IN_WIDTH = OFF_GB + D_MODEL

V7X_VMEM_BYTES = 64 * 1024 * 1024
V7X_LANES = 128
V7X_SUBLANES = 8
VMEM_REQUEST_FLOOR = V7X_VMEM_BYTES * 7 // 8

MIX_TILE = 512
RET_CHUNK = 256
FFN_TILE = 256
FF_SPLIT = 1024

BF16 = jnp.bfloat16
F32 = jnp.float32


def _dot(a, b):
    return jnp.dot(a, b, preferred_element_type=F32)


def _dot_nt(a, b):
    return lax.dot_general(a, b, (((1,), (1,)), ((), ())), preferred_element_type=F32)


def _dot_tn(a, b):
    return lax.dot_general(a, b, (((0,), (0,)), ((), ())), preferred_element_type=F32)


def _rms(x, g):
    return x * lax.rsqrt(jnp.mean(x * x, axis=-1, keepdims=True) + EPS) * g


def _layer_norm(x, g, b):
    mu = jnp.mean(x, axis=-1, keepdims=True)
    xc = x - mu
    return xc * lax.rsqrt(jnp.mean(xc * xc, axis=-1, keepdims=True) + EPS) * g + b


def _rotate(x, cos, sin):
    half = RET_DK // 2
    x1 = x[:, :half]
    x2 = x[:, half:]
    return jnp.concatenate([x1 * cos - x2 * sin, x1 * sin + x2 * cos], axis=-1)


def _head_gammas():
    return tuple(1.0 - 2.0 ** (-5.0 - h) for h in range(RET_HEADS))


def _full(shape):
    n = len(shape)
    return pl.BlockSpec(shape, lambda *_: (0,) * n)


def _window_bytes(shape, dtype):
    itemsize = jnp.dtype(dtype).itemsize
    dims = [d for d in shape if d is not None]
    dims[-1] = -(-dims[-1] // V7X_LANES) * V7X_LANES
    if len(dims) > 1:
        sub = V7X_SUBLANES * 4 // itemsize
        dims[-2] = -(-dims[-2] // sub) * sub
    return int(np.prod(dims)) * itemsize


def _vmem_limit(whole, streamed, temp_tiles):
    need = (sum(_window_bytes(s, d) for s, d in whole)
            + 2 * sum(_window_bytes(s, d) for s, d in streamed)
            + temp_tiles * _window_bytes((RET_CHUNK, D_MODEL), F32))
    assert need <= V7X_VMEM_BYTES, need
    return max(need, VMEM_REQUEST_FLOOR)


def _cast_specs(stacked, layer, steps, step_of):
    _, rows, cols = stacked.shape
    r = rows // steps
    assert r * steps == rows
    src = pl.BlockSpec((None, r, cols), lambda *idx: (layer, step_of(*idx), 0))
    dst = pl.BlockSpec((r, cols), lambda *idx: (step_of(*idx), 0))
    return src, dst


def _cast_rows(cast_in, cast_out):
    for src, dst in zip(cast_in, cast_out):
        dst[...] = src[...].astype(BF16)


def _mixer_kernel(*refs, chunk_decay, n_cast):
    (x_ref, cos_ref, sin_ref, ng_ref, win_ref, bg_ref, lng_ref, lnb_ref, ws_ref, bst_ref,
     intra_ref, qdec_ref, kdec_ref, wpg_ref, wpr_ref, wo_ref) = refs[:16]
    cast_in = refs[16:16 + n_cast]
    y_ref, s_ref = refs[-4 - n_cast:-2 - n_cast]
    cast_out = refs[-2 - n_cast:-2]
    a_scr, r_scr = refs[-2:]
    c = RET_CHUNK
    n_chunks = x_ref.shape[0] // c

    @pl.when(pl.program_id(1) == 0)
    def _():
        s_ref[...] = jnp.zeros(s_ref.shape, F32)

    row = lax.broadcasted_iota(jnp.int32, (GM_CHUNK, GM_CHUNK), 0)
    col = lax.broadcasted_iota(jnp.int32, (GM_CHUNK, GM_CHUNK), 1)

    def chunk_body(i):
        rows = pl.ds(i * c, c)
        x = x_ref[rows, :]
        xn = _rms(x, ng_ref[...]).astype(BF16)

        u = jax.nn.gelu(_dot(xn, win_ref[:, OFF_U:OFF_U + GM_WIDTH]))
        v = jax.nn.gelu(_dot(xn, win_ref[:, OFF_V:OFF_V + GM_WIDTH]))
        ga = jax.nn.sigmoid(_dot(xn, win_ref[:, OFF_GA:OFF_GA + D_MODEL]) + bg_ref[:, :D_MODEL])
        gb = jax.nn.sigmoid(_dot(xn, win_ref[:, OFF_GB:OFF_GB + D_MODEL]) + bg_ref[:, D_MODEL:])

        vb = _layer_norm(v, lng_ref[...], lnb_ref[...]).astype(BF16)
        for g in range(GM_GROUPS):
            w_tril = jnp.where(row >= col, ws_ref[g], 0.0).astype(BF16)
            b_col = bst_ref[:, g:g + 1]
            cs = slice(g * GM_GROUP_DIM, (g + 1) * GM_GROUP_DIM)
            for k in range(c // GM_CHUNK):
                rs = slice(k * GM_CHUNK, (k + 1) * GM_CHUNK)
                mixed = _dot(w_tril, vb[rs, cs]) + b_col
                a_scr[rs, cs] = (u[rs, cs] * mixed).astype(BF16)

        cos = cos_ref[rows, :]
        sin = sin_ref[rows, :]
        for h in range(RET_HEADS):
            q = _dot(xn, win_ref[:, OFF_Q + h * RET_DK:OFF_Q + (h + 1) * RET_DK])
            k = _dot(xn, win_ref[:, OFF_K + h * RET_DK:OFF_K + (h + 1) * RET_DK])
            vr = _dot(xn, win_ref[:, OFF_VR + h * RET_DV:OFF_VR + (h + 1) * RET_DV]).astype(BF16)
            gr = _dot(xn, win_ref[:, OFF_GR + h * RET_DV:OFF_GR + (h + 1) * RET_DV])
            qr = _rotate(q, cos, sin).astype(BF16)
            kr = _rotate(k, cos, sin) * (RET_DK ** -0.5)
            s_old = s_ref[h]
            scores = _dot_nt(qr, kr.astype(BF16)) * intra_ref[h]
            cross = _dot(qr, s_old.astype(BF16)) * qdec_ref[h]
            s_ref[h] = s_old * chunk_decay[h] + _dot_tn((kr * kdec_ref[h]).astype(BF16), vr)
            o = _dot(scores.astype(BF16), vr) + cross
            on = o * lax.rsqrt(jnp.mean(o * o, axis=-1, keepdims=True) + EPS)
            r_scr[:, h * RET_DV:(h + 1) * RET_DV] = (jax.nn.silu(gr) * on).astype(BF16)

        hm = ga * _dot(a_scr[...], wpg_ref[...]) + gb * _dot(r_scr[...], wpr_ref[...])
        y_ref[rows, :] = x + _dot(hm.astype(BF16), wo_ref[...])

        for src, dst in zip(cast_in, cast_out):
            r = src.shape[0] // n_chunks
            part = pl.ds(i * r, r)
            dst[part, :] = src[part, :].astype(BF16)

    for i in range(n_chunks):
        chunk_body(i)


def _retention_constants(c):
    log_g = np.log(np.array(_head_gammas(), np.float64))
    idx = np.arange(c, dtype=np.float64)
    diff = idx[:, None] - idx[None, :]
    intra = np.where(diff[None] >= 0, np.exp(diff[None] * log_g[:, None, None]), 0.0)
    qdec = np.exp((idx[None, :, None] + 1.0) * log_g[:, None, None])
    kdec = np.exp((c - 1.0 - idx[None, :, None]) * log_g[:, None, None])
    chunk_decay = tuple(float(v) for v in np.exp(c * log_g).astype(np.float32))
    return (jnp.asarray(intra, F32), jnp.asarray(qdec, F32), jnp.asarray(kdec, F32), chunk_decay)


def _rope_tables(pos0, length):
    inv_freq = 1.0 / (ROPE_BASE ** jnp.linspace(0.0, 1.0, RET_DK // 2, dtype=F32))
    pos = (jnp.arange(length, dtype=jnp.int32) + pos0).astype(F32)
    ang = pos[:, None] * inv_freq[None, :]
    return jnp.cos(ang), jnp.sin(ang)


def _mixer_call(layer, x, cos, sin, ng, win, bg, lng, lnb, ws, bst, intra, qdec, kdec, wpg, wpr,
                wo, chunk_decay, prev_states, cast_srcs):
    b, l, d = x.shape
    t = MIX_TILE
    per_row = l // t
    steps = b * per_row
    tok = pl.BlockSpec((None, t, d), lambda i, j: (i, j, 0))
    rope = pl.BlockSpec((t, RET_DK // 2), lambda i, j: (j, 0))
    in_specs = [tok, rope, rope, _full(ng.shape), _full(win.shape), _full(bg.shape),
                _full(lng.shape), _full(lnb.shape), _full(ws.shape), _full(bst.shape),
                _full(intra.shape), _full(qdec.shape), _full(kdec.shape),
                _full(wpg.shape), _full(wpr.shape), _full(wo.shape)]
    args = [x, cos, sin, ng, win, bg, lng, lnb, ws, bst, intra, qdec, kdec, wpg, wpr, wo]
    cast_out_specs = []
    cast_out_shapes = []
    for w in cast_srcs:
        src, dst = _cast_specs(w, layer, steps, lambda i, j: i * per_row + j)
        in_specs.append(src)
        args.append(w)
        cast_out_specs.append(dst)
        cast_out_shapes.append(jax.ShapeDtypeStruct(w.shape[1:], BF16))
    aliases = {}
    if prev_states is not None:
        in_specs.append(pl.BlockSpec(memory_space=pl.ANY))
        args.append(prev_states)
        aliases = {len(args) - 1: 1}
    state_spec = pl.BlockSpec((None, None, RET_HEADS, RET_DK, RET_DV),
                              lambda i, j: (layer, i, 0, 0, 0))
    scratch = [((RET_CHUNK, GM_WIDTH), BF16), ((RET_CHUNK, RET_V), BF16)]
    vmem = _vmem_limit(
        whole=[(a.shape, a.dtype) for a in args[3:16]] + scratch,
        streamed=[((t, d), F32)] * 2 + [((t, RET_DK // 2), F32)] * 2
        + [((RET_HEADS, RET_DK, RET_DV), F32)]
        + [(s.block_shape, dt) for s, dt in zip(in_specs[16:16 + len(cast_srcs)],
                                                 [F32] * len(cast_srcs))]
        + [(s.block_shape, BF16) for s in cast_out_specs],
        temp_tiles=10)
    outs = pl.pallas_call(
        functools.partial(_mixer_kernel, chunk_decay=chunk_decay, n_cast=len(cast_srcs)),
        grid=(b, per_row), in_specs=in_specs,
        out_specs=[tok, state_spec] + cast_out_specs,
        out_shape=[jax.ShapeDtypeStruct((b, l, d), F32),
                   jax.ShapeDtypeStruct((DEPTH, b, RET_HEADS, RET_DK, RET_DV), F32)]
        + cast_out_shapes,
        scratch_shapes=[pltpu.VMEM(s, dt) for s, dt in scratch],
        input_output_aliases=aliases,
        compiler_params=pltpu.CompilerParams(
            dimension_semantics=("arbitrary", "arbitrary"), vmem_limit_bytes=vmem),
        name="prompt_mixer",
    )(*args)
    return outs[0], outs[1], tuple(outs[2:])


def _ffn_body(x, hn, wup_ref, wdn_ref, anchor=None, before_block=None):
    acc = x
    for j in range(D_FF // FF_SPLIT):
        cols = slice(j * FF_SPLIT, (j + 1) * FF_SPLIT)
        if before_block is not None:
            before_block(j)
        up = _dot(hn, wup_ref[:, cols])
        if anchor is not None:
            up = up + anchor[:, cols]
        act = jnp.square(jnp.maximum(up, 0.0)).astype(BF16)
        acc = acc + _dot(act, wdn_ref[cols, :])
    return acc


def _zero_of(v):
    bits = pltpu.bitcast(v, jnp.int32)
    bits = lax.shift_right_logical(lax.shift_right_logical(bits, 16), 16)
    return bits.astype(F32)


def _ffn_state_kernel(*refs, final, gammas, seqs, n_cast):
    x_ref, g_ref, wup_ref, wdn_ref, gf_ref, s_ref, qk_ref, v_ref = refs[:8]
    cast_in = refs[8:8 + n_cast]
    y_ref, snew_ref, o_ref = refs[-3 - n_cast:len(refs) - n_cast]
    cast_out = refs[len(refs) - n_cast:]

    _cast_rows(cast_in, cast_out)

    zeros = []
    for j in range(seqs):
        for h in range(RET_HEADS):
            q_col = qk_ref[:, j * RET_HEADS + h:j * RET_HEADS + h + 1]
            k_col = qk_ref[:, (seqs + j) * RET_HEADS + h:(seqs + j) * RET_HEADS + h + 1]
            v_row = v_ref[j:j + 1, h * RET_DV:(h + 1) * RET_DV]
            snew_ref[j, h] = s_ref[j, h] * gammas[h] + k_col * v_row
            o = jnp.sum(q_col * snew_ref[j, h], axis=0, keepdims=True)
            o_ref[j:j + 1, h * RET_DV:(h + 1) * RET_DV] = o
            zeros.append(_zero_of(o))

    anchor = jnp.concatenate(zeros, axis=-1)
    assert anchor.shape == (1, D_FF)
    x = x_ref[...]
    y = _ffn_body(x, _rms(x, g_ref[...]).astype(BF16), wup_ref, wdn_ref, anchor)
    if final:
        y = _rms(y, gf_ref[...])
    y_ref[...] = y


def _ffn_state_call(layer, x2d, g, wup, wdn, gf, final, state_ret, q_s, k_s, v_s, prev_out,
                    cast_srcs):
    tokens, d = x2d.shape
    n = q_s.shape[0]
    t = FFN_TILE
    steps = tokens // t
    seqs = n // steps
    assert steps * t == tokens and seqs * steps == n

    q4 = q_s.reshape(steps, seqs, RET_HEADS, RET_DK)
    k4 = k_s.reshape(steps, seqs, RET_HEADS, RET_DK)
    qk = jnp.stack([q4, k4], axis=0).transpose(1, 4, 0, 2, 3).reshape(
        steps, RET_DK, 2 * seqs * RET_HEADS)
    v3 = v_s.reshape(steps, seqs, RET_V)

    tok = pl.BlockSpec((t, d), lambda i: (i, 0))
    s_spec = pl.BlockSpec((None, seqs, RET_HEADS, RET_DK, RET_DV), lambda i: (layer, i, 0, 0, 0))
    qk_spec = pl.BlockSpec((None, RET_DK, 2 * seqs * RET_HEADS), lambda i: (i, 0, 0))
    v_spec = pl.BlockSpec((None, seqs, RET_V), lambda i: (i, 0, 0))
    in_specs = [tok, _full(g.shape), _full(wup.shape), _full(wdn.shape), _full(gf.shape),
                s_spec, qk_spec, v_spec]
    args = [x2d, g, wup, wdn, gf, state_ret, qk, v3]
    cast_out_specs = []
    cast_out_shapes = []
    for w in cast_srcs:
        src, dst = _cast_specs(w, layer + 1, steps, lambda i: i)
        in_specs.append(src)
        args.append(w)
        cast_out_specs.append(dst)
        cast_out_shapes.append(jax.ShapeDtypeStruct(w.shape[1:], BF16))
    aliases = {}
    if prev_out is not None:
        in_specs.append(pl.BlockSpec(memory_space=pl.ANY))
        args.append(prev_out)
        aliases = {len(args) - 1: 1}
    vmem = _vmem_limit(
        whole=[(a.shape, a.dtype) for a in (g, wup, wdn, gf)],
        streamed=[(tok.block_shape, F32), (s_spec.block_shape, F32)] * 2
        + [(qk_spec.block_shape, F32)] + [(v_spec.block_shape, F32)] * 2
        + [(s.block_shape, F32) for s in in_specs[8:8 + len(cast_srcs)]]
        + [(s.block_shape, BF16) for s in cast_out_specs],
        temp_tiles=6)
    outs = pl.pallas_call(
        functools.partial(_ffn_state_kernel, final=final, gammas=_head_gammas(), seqs=seqs,
                          n_cast=len(cast_srcs)),
        grid=(steps,), in_specs=in_specs,
        out_specs=[tok, s_spec, v_spec] + cast_out_specs,
        out_shape=[jax.ShapeDtypeStruct((tokens, d), F32),
                   jax.ShapeDtypeStruct(state_ret.shape, F32),
                   jax.ShapeDtypeStruct((steps, seqs, RET_V), F32)] + cast_out_shapes,
        input_output_aliases=aliases,
        compiler_params=pltpu.CompilerParams(
            dimension_semantics=("arbitrary",), vmem_limit_bytes=vmem),
        name="prompt_ffn_sample_state",
    )(*args)
    return outs[0], outs[1], outs[2].reshape(n, RET_V), tuple(outs[3:])


COPY_DEPTH = 2


def _copy_window(copies):
    for cp in copies[:COPY_DEPTH]:
        cp.start()

    def wait(i):
        copies[i].wait()
        if i + COPY_DEPTH < len(copies):
            copies[i + COPY_DEPTH].start()

    return wait


PRE_SECTION = 2 * GM_WIDTH
assert OFF_Q == PRE_SECTION and OFF_VR == 2 * PRE_SECTION and OFF_GR == 3 * PRE_SECTION
assert OFF_GA == 4 * PRE_SECTION and IN_WIDTH == 5 * PRE_SECTION


def _sample_pre_kernel(x_ref, cos_ref, sin_ref, ng_ref, bg_ref, lng_ref, lnb_ref, wsd_ref,
                       bsd_ref, win_hbm, v_ref, a_ref, q_ref, k_ref, vr_ref, sg_ref, gate_ref,
                       w_scr, sems):
    copies = [pltpu.make_async_copy(win_hbm.at[:, pl.ds(s * PRE_SECTION, PRE_SECTION)],
                                    w_scr.at[s], sems.at[s])
              for s in range(IN_WIDTH // PRE_SECTION)]
    wait = _copy_window(copies)
    xn = _rms(x_ref[...], ng_ref[...]).astype(BF16)

    wait(0)
    u = jax.nn.gelu(_dot(xn, w_scr[0, :, :GM_WIDTH]))
    v = jax.nn.gelu(_dot(xn, w_scr[0, :, GM_WIDTH:]))
    v = _layer_norm(v, lng_ref[...], lnb_ref[...])
    v_ref[...] = v
    a_ref[...] = (u * (wsd_ref[...] * v + bsd_ref[...])).astype(BF16)

    wait(1)
    cos = cos_ref[...]
    sin = sin_ref[...]
    for h in range(RET_HEADS):
        qs = slice(h * RET_DK, (h + 1) * RET_DK)
        q = _dot(xn, w_scr[1, :, h * RET_DK:(h + 1) * RET_DK])
        k = _dot(xn, w_scr[1, :, RET_QK + h * RET_DK:RET_QK + (h + 1) * RET_DK])
        q_ref[:, qs] = _rotate(q, cos, sin)
        k_ref[:, qs] = _rotate(k, cos, sin) * (RET_DK ** -0.5)
    wait(2)
    vr_ref[...] = _dot(xn, w_scr[2])
    wait(3)
    sg_ref[...] = jax.nn.silu(_dot(xn, w_scr[3]))
    wait(4)
    gate_ref[...] = jax.nn.sigmoid(_dot(xn, w_scr[4]) + bg_ref[...])


def _sample_pre_call(x, cos, sin, ng, win, bg, lng, lnb, wsd, bsd):
    n = x.shape[0]
    out_shape = [jax.ShapeDtypeStruct((n, GM_WIDTH), F32),
                 jax.ShapeDtypeStruct((n, GM_WIDTH), BF16),
                 jax.ShapeDtypeStruct((n, RET_QK), F32),
                 jax.ShapeDtypeStruct((n, RET_QK), F32),
                 jax.ShapeDtypeStruct((n, RET_V), F32),
                 jax.ShapeDtypeStruct((n, RET_V), F32),
                 jax.ShapeDtypeStruct((n, 2 * D_MODEL), F32)]
    args = [x, cos, sin, ng, bg, lng, lnb, wsd, bsd]
    n_sec = IN_WIDTH // PRE_SECTION
    w_scr = ((n_sec, D_MODEL, PRE_SECTION), BF16)
    vmem = _vmem_limit(whole=[(a.shape, a.dtype) for a in args + out_shape] + [w_scr],
                       streamed=[], temp_tiles=8)
    return pl.pallas_call(
        _sample_pre_kernel, grid=(1,),
        in_specs=[_full(a.shape) for a in args] + [pl.BlockSpec(memory_space=pl.ANY)],
        out_specs=[_full(s.shape) for s in out_shape], out_shape=out_shape,
        scratch_shapes=[pltpu.VMEM(*w_scr), pltpu.SemaphoreType.DMA((n_sec,))],
        compiler_params=pltpu.CompilerParams(
            dimension_semantics=("arbitrary",), vmem_limit_bytes=vmem),
        name="sample_pre",
    )(*args, win)


def _sample_post_kernel(x_ref, o_ref, sg_ref, gate_ref, a_ref, nfg_ref, gf_ref,
                        wpg_hbm, wpr_hbm, wo_hbm, wup_hbm, wdn_hbm, y_ref,
                        wpg_v, wpr_v, wo_v, wup_v, wdn_v, sems, *, final):
    moves = [(wpg_hbm, wpg_v), (wpr_hbm, wpr_v), (wo_hbm, wo_v)]
    for j in range(D_FF // FF_SPLIT):
        blk = pl.ds(j * FF_SPLIT, FF_SPLIT)
        moves.append((wup_hbm.at[:, blk], wup_v.at[:, blk]))
        moves.append((wdn_hbm.at[blk, :], wdn_v.at[blk, :]))
    wait = _copy_window(
        [pltpu.make_async_copy(src, dst, sems.at[i]) for i, (src, dst) in enumerate(moves)])

    rs = []
    for h in range(RET_HEADS):
        hs = slice(h * RET_DV, (h + 1) * RET_DV)
        o = o_ref[:, hs]
        on = o * lax.rsqrt(jnp.mean(o * o, axis=-1, keepdims=True) + EPS)
        rs.append((sg_ref[:, hs] * on).astype(BF16))
    r = jnp.concatenate(rs, axis=-1)
    wait(0)
    hm = gate_ref[:, :D_MODEL] * _dot(a_ref[...], wpg_v[...])
    wait(1)
    hm = hm + gate_ref[:, D_MODEL:] * _dot(r, wpr_v[...])
    wait(2)
    x = x_ref[...] + _dot(hm.astype(BF16), wo_v[...])

    def wait_block(j):
        wait(3 + 2 * j)
        wait(4 + 2 * j)

    y = _ffn_body(x, _rms(x, nfg_ref[...]).astype(BF16), wup_v, wdn_v, before_block=wait_block)
    if final:
        y = _rms(y, gf_ref[...])
    y_ref[...] = y


def _sample_post_call(x, o, sg, gate, a, wpg, wpr, wo, nfg, wup, wdn, gf, final):
    args = [x, o, sg, gate, a, nfg, gf]
    weights = [wpg, wpr, wo, wup, wdn]
    n_copies = 3 + 2 * (D_FF // FF_SPLIT)
    vmem = _vmem_limit(whole=[(v.shape, v.dtype) for v in args + weights + [x]], streamed=[],
                       temp_tiles=8)
    return pl.pallas_call(
        functools.partial(_sample_post_kernel, final=final),
        grid=(1,),
        in_specs=[_full(v.shape) for v in args] + [pl.BlockSpec(memory_space=pl.ANY)] * len(weights),
        out_specs=_full(x.shape),
        out_shape=jax.ShapeDtypeStruct(x.shape, F32),
        scratch_shapes=[pltpu.VMEM(w.shape, w.dtype) for w in weights]
        + [pltpu.SemaphoreType.DMA((n_copies,))],
        compiler_params=pltpu.CompilerParams(
            dimension_semantics=("arbitrary",), vmem_limit_bytes=vmem),
        name="sample_post",
    )(*args, *weights)


def kernel(x_prompt, x_sample, state_ret, norm_mix_g, w_in, b_gate, gm_ln_g, gm_ln_b, gm_w_s, gm_b_s,
           w_proj_gm, w_proj_ret, w_out, norm_ffn_g, w_up, w_down, norm_final_g):
    b, l, d = x_prompt.shape
    n = x_sample.shape[0]
    assert l % MIX_TILE == 0 and MIX_TILE % RET_CHUNK == 0 and x_sample.shape[1] == 1

    intra, qdec, kdec, chunk_decay = _retention_constants(RET_CHUNK)
    cos_p, sin_p = _rope_tables(0, l)
    cos_s, sin_s = _rope_tables(PAST_LEN, 1)
    row = lambda a: a.reshape(1, -1)
    gf = row(norm_final_g)

    mixer_f32 = (w_in, w_proj_gm, w_proj_ret, w_out)
    ffn_f32 = (w_up, w_down)
    win, wpg, wpr, wo = (w[0].astype(BF16) for w in mixer_f32)

    xp = x_prompt
    xs = x_sample.reshape(n, d)
    v_rows = []
    prompt_states = None
    sample_states = None
    for layer in range(DEPTH):
        ng = row(norm_mix_g[layer])
        bg = row(b_gate[layer])
        lng = row(gm_ln_g[layer])
        lnb = row(gm_ln_b[layer])
        nfg = row(norm_ffn_g[layer])
        final = layer == DEPTH - 1

        wsd = row(jnp.repeat(gm_w_s[layer, :, 0, 0], GM_GROUP_DIM))
        bsd = row(jnp.repeat(gm_b_s[layer, :, 0], GM_GROUP_DIM))
        v_s, a_s, q_s, k_s, vr_s, sg_s, gate_s = _sample_pre_call(
            xs, cos_s, sin_s, ng, win, bg, lng, lnb, wsd, bsd)
        v_rows.append(v_s)

        xp, prompt_states, (wup, wdn) = _mixer_call(
            layer, xp, cos_p, sin_p, ng, win, bg, lng, lnb, gm_w_s[layer], gm_b_s[layer].T,
            intra, qdec, kdec, wpg, wpr, wo, chunk_decay, prompt_states, ffn_f32)
        xp, sample_states, o_s, next_mixer = _ffn_state_call(
            layer, xp.reshape(b * l, d), nfg, wup, wdn, gf, final, state_ret, q_s, k_s, vr_s,
            sample_states, () if final else mixer_f32)
        xp = xp.reshape(b, l, d)
        xs = _sample_post_call(xs, o_s, sg_s, gate_s, a_s, wpg, wpr, wo, nfg, wup, wdn, gf, final)
        if not final:
            win, wpg, wpr, wo = next_mixer

    return (xp, xs.reshape(n, 1, d), prompt_states, sample_states,
            jnp.stack(v_rows).reshape(DEPTH, n, 1, GM_WIDTH))
```

```python
import functools

import numpy as np
import jax
import jax.numpy as jnp
from jax import lax
from jax.experimental import pallas as pl
from jax.experimental.pallas import tpu as pltpu

D_MODEL = 1024
DEPTH = 4
PAST_LEN = 16384
GM_WIDTH = D_MODEL
GM_GROUPS = 4
GM_GROUP_DIM = GM_WIDTH // GM_GROUPS
GM_CHUNK = 128
RET_HEADS = 4
RET_DK = D_MODEL // RET_HEADS
RET_DV = 2 * RET_DK
RET_QK = RET_HEADS * RET_DK
RET_V = RET_HEADS * RET_DV
ROPE_BASE = 10000.0
D_FF = 4 * D_MODEL
EPS = 1e-6

OFF_U = 0
OFF_V = GM_WIDTH
OFF_Q = 2 * GM_WIDTH
OFF_K = OFF_Q + RET_QK
OFF_VR = OFF_K + RET_QK
OFF_GR = OFF_VR + RET_V
OFF_GA = OFF_GR + RET_V
OFF_GB = OFF_GA + D_MODEL
IN_WIDTH = OFF_GB + D_MODEL

V7X_VMEM_BYTES = 64 * 1024 * 1024
V7X_LANES = 128
V7X_SUBLANES = 8
VMEM_REQUEST_FLOOR = V7X_VMEM_BYTES * 7 // 8

MIX_TILE = 512
RET_CHUNK = 256
FFN_TILE = 256
FF_SPLIT = 1024

BF16 = jnp.bfloat16
F32 = jnp.float32


def _dot(a, b):
    return jnp.dot(a, b, preferred_element_type=F32)


def _dot_nt(a, b):
    return lax.dot_general(a, b, (((1,), (1,)), ((), ())), preferred_element_type=F32)


def _dot_tn(a, b):
    return lax.dot_general(a, b, (((0,), (0,)), ((), ())), preferred_element_type=F32)


def _rms(x, g):
    return x * lax.rsqrt(jnp.mean(x * x, axis=-1, keepdims=True) + EPS) * g


def _layer_norm(x, g, b):
    mu = jnp.mean(x, axis=-1, keepdims=True)
    xc = x - mu
    return xc * lax.rsqrt(jnp.mean(xc * xc, axis=-1, keepdims=True) + EPS) * g + b


def _rotate(x, cos, sin):
    half = RET_DK // 2
    x1 = x[:, :half]
    x2 = x[:, half:]
    return jnp.concatenate([x1 * cos - x2 * sin, x1 * sin + x2 * cos], axis=-1)


def _head_gammas():
    return tuple(1.0 - 2.0 ** (-5.0 - h) for h in range(RET_HEADS))


def _full(shape):
    n = len(shape)
    return pl.BlockSpec(shape, lambda *_: (0,) * n)


def _window_bytes(shape, dtype):
    itemsize = jnp.dtype(dtype).itemsize
    dims = [d for d in shape if d is not None]
    dims[-1] = -(-dims[-1] // V7X_LANES) * V7X_LANES
    if len(dims) > 1:
        sub = V7X_SUBLANES * 4 // itemsize
        dims[-2] = -(-dims[-2] // sub) * sub
    return int(np.prod(dims)) * itemsize


def _vmem_limit(whole, streamed, temp_tiles):
    need = (sum(_window_bytes(s, d) for s, d in whole)
            + 2 * sum(_window_bytes(s, d) for s, d in streamed)
            + temp_tiles * _window_bytes((RET_CHUNK, D_MODEL), F32))
    assert need <= V7X_VMEM_BYTES, need
    return max(need, VMEM_REQUEST_FLOOR)


def _cast_specs(stacked, layer, steps, step_of):
    _, rows, cols = stacked.shape
    r = rows // steps
    assert r * steps == rows
    src = pl.BlockSpec((None, r, cols), lambda *idx: (layer, step_of(*idx), 0))
    dst = pl.BlockSpec((r, cols), lambda *idx: (step_of(*idx), 0))
    return src, dst


def _cast_rows(cast_in, cast_out):
    for src, dst in zip(cast_in, cast_out):
        dst[...] = src[...].astype(BF16)


def _mixer_kernel(*refs, chunk_decay, n_cast):
    (x_ref, cos_ref, sin_ref, ng_ref, win_ref, bg_ref, lng_ref, lnb_ref, ws_ref, bst_ref,
     intra_ref, qdec_ref, kdec_ref, wpg_ref, wpr_ref, wo_ref) = refs[:16]
    cast_in = refs[16:16 + n_cast]
    y_ref, s_ref = refs[-4 - n_cast:-2 - n_cast]
    cast_out = refs[-2 - n_cast:-2]
    a_scr, r_scr = refs[-2:]
    c = RET_CHUNK
    n_chunks = x_ref.shape[0] // c

    @pl.when(pl.program_id(1) == 0)
    def _():
        s_ref[...] = jnp.zeros(s_ref.shape, F32)

    row = lax.broadcasted_iota(jnp.int32, (GM_CHUNK, GM_CHUNK), 0)
    col = lax.broadcasted_iota(jnp.int32, (GM_CHUNK, GM_CHUNK), 1)

    def chunk_body(i):
        rows = pl.ds(i * c, c)
        x = x_ref[rows, :]
        xn = _rms(x, ng_ref[...]).astype(BF16)

        u = jax.nn.gelu(_dot(xn, win_ref[:, OFF_U:OFF_U + GM_WIDTH]))
        v = jax.nn.gelu(_dot(xn, win_ref[:, OFF_V:OFF_V + GM_WIDTH]))
        ga = jax.nn.sigmoid(_dot(xn, win_ref[:, OFF_GA:OFF_GA + D_MODEL]) + bg_ref[:, :D_MODEL])
        gb = jax.nn.sigmoid(_dot(xn, win_ref[:, OFF_GB:OFF_GB + D_MODEL]) + bg_ref[:, D_MODEL:])

        vb = _layer_norm(v, lng_ref[...], lnb_ref[...]).astype(BF16)
        for g in range(GM_GROUPS):
            w_tril = jnp.where(row >= col, ws_ref[g], 0.0).astype(BF16)
            b_col = bst_ref[:, g:g + 1]
            cs = slice(g * GM_GROUP_DIM, (g + 1) * GM_GROUP_DIM)
            for k in range(c // GM_CHUNK):
                rs = slice(k * GM_CHUNK, (k + 1) * GM_CHUNK)
                mixed = _dot(w_tril, vb[rs, cs]) + b_col
                a_scr[rs, cs] = (u[rs, cs] * mixed).astype(BF16)

        cos = cos_ref[rows, :]
        sin = sin_ref[rows, :]
        for h in range(RET_HEADS):
            q = _dot(xn, win_ref[:, OFF_Q + h * RET_DK:OFF_Q + (h + 1) * RET_DK])
            k = _dot(xn, win_ref[:, OFF_K + h * RET_DK:OFF_K + (h + 1) * RET_DK])
            vr = _dot(xn, win_ref[:, OFF_VR + h * RET_DV:OFF_VR + (h + 1) * RET_DV]).astype(BF16)
            gr = _dot(xn, win_ref[:, OFF_GR + h * RET_DV:OFF_GR + (h + 1) * RET_DV])
            qr = _rotate(q, cos, sin).astype(BF16)
            kr = _rotate(k, cos, sin) * (RET_DK ** -0.5)
            s_old = s_ref[h]
            scores = _dot_nt(qr, kr.astype(BF16)) * intra_ref[h]
            cross = _dot(qr, s_old.astype(BF16)) * qdec_ref[h]
            s_ref[h] = s_old * chunk_decay[h] + _dot_tn((kr * kdec_ref[h]).astype(BF16), vr)
            o = _dot(scores.astype(BF16), vr) + cross
            on = o * lax.rsqrt(jnp.mean(o * o, axis=-1, keepdims=True) + EPS)
            r_scr[:, h * RET_DV:(h + 1) * RET_DV] = (jax.nn.silu(gr) * on).astype(BF16)

        hm = ga * _dot(a_scr[...], wpg_ref[...]) + gb * _dot(r_scr[...], wpr_ref[...])
        y_ref[rows, :] = x + _dot(hm.astype(BF16), wo_ref[...])

        for src, dst in zip(cast_in, cast_out):
            r = src.shape[0] // n_chunks
            part = pl.ds(i * r, r)
            dst[part, :] = src[part, :].astype(BF16)

    for i in range(n_chunks):
        chunk_body(i)


def _retention_constants(c):
    log_g = np.log(np.array(_head_gammas(), np.float64))
    idx = np.arange(c, dtype=np.float64)
    diff = idx[:, None] - idx[None, :]
    intra = np.where(diff[None] >= 0, np.exp(diff[None] * log_g[:, None, None]), 0.0)
    qdec = np.exp((idx[None, :, None] + 1.0) * log_g[:, None, None])
    kdec = np.exp((c - 1.0 - idx[None, :, None]) * log_g[:, None, None])
    chunk_decay = tuple(float(v) for v in np.exp(c * log_g).astype(np.float32))
    return (jnp.asarray(intra, F32), jnp.asarray(qdec, F32), jnp.asarray(kdec, F32), chunk_decay)


def _rope_tables(pos0, length):
    inv_freq = 1.0 / (ROPE_BASE ** jnp.linspace(0.0, 1.0, RET_DK // 2, dtype=F32))
    pos = (jnp.arange(length, dtype=jnp.int32) + pos0).astype(F32)
    ang = pos[:, None] * inv_freq[None, :]
    return jnp.cos(ang), jnp.sin(ang)


def _mixer_call(layer, x, cos, sin, ng, win, bg, lng, lnb, ws, bst, intra, qdec, kdec, wpg, wpr,
                wo, chunk_decay, prev_states, cast_srcs):
    b, l, d = x.shape
    t = MIX_TILE
    per_row = l // t
    steps = b * per_row
    tok = pl.BlockSpec((None, t, d), lambda i, j: (i, j, 0))
    rope = pl.BlockSpec((t, RET_DK // 2), lambda i, j: (j, 0))
    in_specs = [tok, rope, rope, _full(ng.shape), _full(win.shape), _full(bg.shape),
                _full(lng.shape), _full(lnb.shape), _full(ws.shape), _full(bst.shape),
                _full(intra.shape), _full(qdec.shape), _full(kdec.shape),
                _full(wpg.shape), _full(wpr.shape), _full(wo.shape)]
    args = [x, cos, sin, ng, win, bg, lng, lnb, ws, bst, intra, qdec, kdec, wpg, wpr, wo]
    cast_out_specs = []
    cast_out_shapes = []
    for w in cast_srcs:
        src, dst = _cast_specs(w, layer, steps, lambda i, j: i * per_row + j)
        in_specs.append(src)
        args.append(w)
        cast_out_specs.append(dst)
        cast_out_shapes.append(jax.ShapeDtypeStruct(w.shape[1:], BF16))
    aliases = {}
    if prev_states is not None:
        in_specs.append(pl.BlockSpec(memory_space=pl.ANY))
        args.append(prev_states)
        aliases = {len(args) - 1: 1}
    state_spec = pl.BlockSpec((None, None, RET_HEADS, RET_DK, RET_DV),
                              lambda i, j: (layer, i, 0, 0, 0))
    scratch = [((RET_CHUNK, GM_WIDTH), BF16), ((RET_CHUNK, RET_V), BF16)]
    vmem = _vmem_limit(
        whole=[(a.shape, a.dtype) for a in args[3:16]] + scratch,
        streamed=[((t, d), F32)] * 2 + [((t, RET_DK // 2), F32)] * 2
        + [((RET_HEADS, RET_DK, RET_DV), F32)]
        + [(s.block_shape, dt) for s, dt in zip(in_specs[16:16 + len(cast_srcs)],
                                                 [F32] * len(cast_srcs))]
        + [(s.block_shape, BF16) for s in cast_out_specs],
        temp_tiles=10)
    outs = pl.pallas_call(
        functools.partial(_mixer_kernel, chunk_decay=chunk_decay, n_cast=len(cast_srcs)),
        grid=(b, per_row), in_specs=in_specs,
        out_specs=[tok, state_spec] + cast_out_specs,
        out_shape=[jax.ShapeDtypeStruct((b, l, d), F32),
                   jax.ShapeDtypeStruct((DEPTH, b, RET_HEADS, RET_DK, RET_DV), F32)]
        + cast_out_shapes,
        scratch_shapes=[pltpu.VMEM(s, dt) for s, dt in scratch],
        input_output_aliases=aliases,
        compiler_params=pltpu.CompilerParams(
            dimension_semantics=("arbitrary", "arbitrary"), vmem_limit_bytes=vmem),
        name="prompt_mixer",
    )(*args)
    return outs[0], outs[1], tuple(outs[2:])


def _ffn_body(x, hn, wup_ref, wdn_ref, anchor=None):
    acc = x
    for j in range(D_FF // FF_SPLIT):
        cols = slice(j * FF_SPLIT, (j + 1) * FF_SPLIT)
        up = _dot(hn, wup_ref[:, cols])
        if anchor is not None:
            up = up + anchor[:, cols]
        act = jnp.square(jnp.maximum(up, 0.0)).astype(BF16)
        acc = acc + _dot(act, wdn_ref[cols, :])
    return acc


def _zero_of(v):
    bits = pltpu.bitcast(v, jnp.int32)
    bits = lax.shift_right_logical(lax.shift_right_logical(bits, 16), 16)
    return bits.astype(F32)


def _ffn_state_kernel(*refs, final, gammas, seqs, n_cast):
    x_ref, g_ref, wup_ref, wdn_ref, gf_ref, s_ref, qk_ref, v_ref = refs[:8]
    cast_in = refs[8:8 + n_cast]
    y_ref, snew_ref, o_ref = refs[-3 - n_cast:len(refs) - n_cast]
    cast_out = refs[len(refs) - n_cast:]

    _cast_rows(cast_in, cast_out)

    zeros = []
    for j in range(seqs):
        for h in range(RET_HEADS):
            q_col = qk_ref[:, j * RET_HEADS + h:j * RET_HEADS + h + 1]
            k_col = qk_ref[:, (seqs + j) * RET_HEADS + h:(seqs + j) * RET_HEADS + h + 1]
            v_row = v_ref[j:j + 1, h * RET_DV:(h + 1) * RET_DV]
            snew_ref[j, h] = s_ref[j, h] * gammas[h] + k_col * v_row
            o = jnp.sum(q_col * snew_ref[j, h], axis=0, keepdims=True)
            o_ref[j:j + 1, h * RET_DV:(h + 1) * RET_DV] = o
            zeros.append(_zero_of(o))

    anchor = jnp.concatenate(zeros, axis=-1)
    assert anchor.shape == (1, D_FF)
    x = x_ref[...]
    y = _ffn_body(x, _rms(x, g_ref[...]).astype(BF16), wup_ref, wdn_ref, anchor)
    if final:
        y = _rms(y, gf_ref[...])
    y_ref[...] = y


def _ffn_state_call(layer, x2d, g, wup, wdn, gf, final, state_ret, q_s, k_s, v_s, prev_out,
                    cast_srcs):
    tokens, d = x2d.shape
    n = q_s.shape[0]
    t = FFN_TILE
    steps = tokens // t
    seqs = n // steps
    assert steps * t == tokens and seqs * steps == n

    q4 = q_s.reshape(steps, seqs, RET_HEADS, RET_DK)
    k4 = k_s.reshape(steps, seqs, RET_HEADS, RET_DK)
    qk = jnp.stack([q4, k4], axis=0).transpose(1, 4, 0, 2, 3).reshape(
        steps, RET_DK, 2 * seqs * RET_HEADS)
    v3 = v_s.reshape(steps, seqs, RET_V)

    tok = pl.BlockSpec((t, d), lambda i: (i, 0))
    s_spec = pl.BlockSpec((None, seqs, RET_HEADS, RET_DK, RET_DV), lambda i: (layer, i, 0, 0, 0))
    qk_spec = pl.BlockSpec((None, RET_DK, 2 * seqs * RET_HEADS), lambda i: (i, 0, 0))
    v_spec = pl.BlockSpec((None, seqs, RET_V), lambda i: (i, 0, 0))
    in_specs = [tok, _full(g.shape), _full(wup.shape), _full(wdn.shape), _full(gf.shape),
                s_spec, qk_spec, v_spec]
    args = [x2d, g, wup, wdn, gf, state_ret, qk, v3]
    cast_out_specs = []
    cast_out_shapes = []
    for w in cast_srcs:
        src, dst = _cast_specs(w, layer + 1, steps, lambda i: i)
        in_specs.append(src)
        args.append(w)
        cast_out_specs.append(dst)
        cast_out_shapes.append(jax.ShapeDtypeStruct(w.shape[1:], BF16))
    aliases = {}
    if prev_out is not None:
        in_specs.append(pl.BlockSpec(memory_space=pl.ANY))
        args.append(prev_out)
        aliases = {len(args) - 1: 1}
    vmem = _vmem_limit(
        whole=[(a.shape, a.dtype) for a in (g, wup, wdn, gf)],
        streamed=[(tok.block_shape, F32), (s_spec.block_shape, F32)] * 2
        + [(qk_spec.block_shape, F32)] + [(v_spec.block_shape, F32)] * 2
        + [(s.block_shape, F32) for s in in_specs[8:8 + len(cast_srcs)]]
        + [(s.block_shape, BF16) for s in cast_out_specs],
        temp_tiles=6)
    outs = pl.pallas_call(
        functools.partial(_ffn_state_kernel, final=final, gammas=_head_gammas(), seqs=seqs,
                          n_cast=len(cast_srcs)),
        grid=(steps,), in_specs=in_specs,
        out_specs=[tok, s_spec, v_spec] + cast_out_specs,
        out_shape=[jax.ShapeDtypeStruct((tokens, d), F32),
                   jax.ShapeDtypeStruct(state_ret.shape, F32),
                   jax.ShapeDtypeStruct((steps, seqs, RET_V), F32)] + cast_out_shapes,
        input_output_aliases=aliases,
        compiler_params=pltpu.CompilerParams(
            dimension_semantics=("arbitrary",), vmem_limit_bytes=vmem),
        name="prompt_ffn_sample_state",
    )(*args)
    return outs[0], outs[1], outs[2].reshape(n, RET_V), tuple(outs[3:])


def _sample_pre_kernel(x_ref, cos_ref, sin_ref, ng_ref, win_ref, bg_ref, lng_ref, lnb_ref,
                       wsd_ref, bsd_ref, v_ref, a_ref, q_ref, k_ref, vr_ref, sg_ref, gate_ref):
    xn = _rms(x_ref[...], ng_ref[...]).astype(BF16)
    u = jax.nn.gelu(_dot(xn, win_ref[:, OFF_U:OFF_U + GM_WIDTH]))
    v = jax.nn.gelu(_dot(xn, win_ref[:, OFF_V:OFF_V + GM_WIDTH]))
    v = _layer_norm(v, lng_ref[...], lnb_ref[...])
    v_ref[...] = v
    a_ref[...] = (u * (wsd_ref[...] * v + bsd_ref[...])).astype(BF16)
    cos = cos_ref[...]
    sin = sin_ref[...]
    for h in range(RET_HEADS):
        qs = slice(h * RET_DK, (h + 1) * RET_DK)
        q = _dot(xn, win_ref[:, OFF_Q + h * RET_DK:OFF_Q + (h + 1) * RET_DK])
        k = _dot(xn, win_ref[:, OFF_K + h * RET_DK:OFF_K + (h + 1) * RET_DK])
        q_ref[:, qs] = _rotate(q, cos, sin)
        k_ref[:, qs] = _rotate(k, cos, sin) * (RET_DK ** -0.5)
    vr_ref[...] = _dot(xn, win_ref[:, OFF_VR:OFF_VR + RET_V])
    sg_ref[...] = jax.nn.silu(_dot(xn, win_ref[:, OFF_GR:OFF_GR + RET_V]))
    gate_ref[...] = jax.nn.sigmoid(_dot(xn, win_ref[:, OFF_GA:OFF_GA + 2 * D_MODEL]) + bg_ref[...])


def _sample_pre_call(x, cos, sin, ng, win, bg, lng, lnb, wsd, bsd):
    n = x.shape[0]
    out_shape = [jax.ShapeDtypeStruct((n, GM_WIDTH), F32),
                 jax.ShapeDtypeStruct((n, GM_WIDTH), BF16),
                 jax.ShapeDtypeStruct((n, RET_QK), F32),
                 jax.ShapeDtypeStruct((n, RET_QK), F32),
                 jax.ShapeDtypeStruct((n, RET_V), F32),
                 jax.ShapeDtypeStruct((n, RET_V), F32),
                 jax.ShapeDtypeStruct((n, 2 * D_MODEL), F32)]
    args = [x, cos, sin, ng, win, bg, lng, lnb, wsd, bsd]
    vmem = _vmem_limit(whole=[(a.shape, a.dtype) for a in args + out_shape], streamed=[],
                       temp_tiles=8)
    return pl.pallas_call(
        _sample_pre_kernel, grid=(1,), in_specs=[_full(a.shape) for a in args],
        out_specs=[_full(s.shape) for s in out_shape], out_shape=out_shape,
        compiler_params=pltpu.CompilerParams(
            dimension_semantics=("arbitrary",), vmem_limit_bytes=vmem),
        name="sample_pre",
    )(*args)


def _sample_post_kernel(x_ref, o_ref, sg_ref, gate_ref, a_ref, wpg_ref, wpr_ref, wo_ref,
                        nfg_ref, wup_ref, wdn_ref, gf_ref, y_ref, *, final):
    rs = []
    for h in range(RET_HEADS):
        hs = slice(h * RET_DV, (h + 1) * RET_DV)
        o = o_ref[:, hs]
        on = o * lax.rsqrt(jnp.mean(o * o, axis=-1, keepdims=True) + EPS)
        rs.append((sg_ref[:, hs] * on).astype(BF16))
    r = jnp.concatenate(rs, axis=-1)
    hm = (gate_ref[:, :D_MODEL] * _dot(a_ref[...], wpg_ref[...])
          + gate_ref[:, D_MODEL:] * _dot(r, wpr_ref[...]))
    x = x_ref[...] + _dot(hm.astype(BF16), wo_ref[...])
    y = _ffn_body(x, _rms(x, nfg_ref[...]).astype(BF16), wup_ref, wdn_ref)
    if final:
        y = _rms(y, gf_ref[...])
    y_ref[...] = y


def _sample_post_call(x, o, sg, gate, a, wpg, wpr, wo, nfg, wup, wdn, gf, final):
    args = [x, o, sg, gate, a, wpg, wpr, wo, nfg, wup, wdn, gf]
    vmem = _vmem_limit(whole=[(v.shape, v.dtype) for v in args + [x]], streamed=[], temp_tiles=8)
    return pl.pallas_call(
        functools.partial(_sample_post_kernel, final=final),
        grid=(1,), in_specs=[_full(v.shape) for v in args], out_specs=_full(x.shape),
        out_shape=jax.ShapeDtypeStruct(x.shape, F32),
        compiler_params=pltpu.CompilerParams(
            dimension_semantics=("arbitrary",), vmem_limit_bytes=vmem),
        name="sample_post",
    )(*args)


def kernel(x_prompt, x_sample, state_ret, norm_mix_g, w_in, b_gate, gm_ln_g, gm_ln_b, gm_w_s, gm_b_s,
           w_proj_gm, w_proj_ret, w_out, norm_ffn_g, w_up, w_down, norm_final_g):
    b, l, d = x_prompt.shape
    n = x_sample.shape[0]
    assert l % MIX_TILE == 0 and MIX_TILE % RET_CHUNK == 0 and x_sample.shape[1] == 1

    intra, qdec, kdec, chunk_decay = _retention_constants(RET_CHUNK)
    cos_p, sin_p = _rope_tables(0, l)
    cos_s, sin_s = _rope_tables(PAST_LEN, 1)
    row = lambda a: a.reshape(1, -1)
    gf = row(norm_final_g)

    mixer_f32 = (w_in, w_proj_gm, w_proj_ret, w_out)
    ffn_f32 = (w_up, w_down)
    win, wpg, wpr, wo = (w[0].astype(BF16) for w in mixer_f32)

    xp = x_prompt
    xs = x_sample.reshape(n, d)
    v_rows = []
    prompt_states = None
    sample_states = None
    for layer in range(DEPTH):
        ng = row(norm_mix_g[layer])
        bg = row(b_gate[layer])
        lng = row(gm_ln_g[layer])
        lnb = row(gm_ln_b[layer])
        nfg = row(norm_ffn_g[layer])
        final = layer == DEPTH - 1

        wsd = row(jnp.repeat(gm_w_s[layer, :, 0, 0], GM_GROUP_DIM))
        bsd = row(jnp.repeat(gm_b_s[layer, :, 0], GM_GROUP_DIM))
        v_s, a_s, q_s, k_s, vr_s, sg_s, gate_s = _sample_pre_call(
            xs, cos_s, sin_s, ng, win, bg, lng, lnb, wsd, bsd)
        v_rows.append(v_s)

        xp, prompt_states, (wup, wdn) = _mixer_call(
            layer, xp, cos_p, sin_p, ng, win, bg, lng, lnb, gm_w_s[layer], gm_b_s[layer].T,
            intra, qdec, kdec, wpg, wpr, wo, chunk_decay, prompt_states, ffn_f32)
        xp, sample_states, o_s, next_mixer = _ffn_state_call(
            layer, xp.reshape(b * l, d), nfg, wup, wdn, gf, final, state_ret, q_s, k_s, vr_s,
            sample_states, () if final else mixer_f32)
        xp = xp.reshape(b, l, d)
        xs = _sample_post_call(xs, o_s, sg_s, gate_s, a_s, wpg, wpr, wo, nfg, wup, wdn, gf, final)
        if not final:
            win, wpg, wpr, wo = next_mixer

    return (xp, xs.reshape(n, 1, d), prompt_states, sample_states,
            jnp.stack(v_rows).reshape(DEPTH, n, 1, GM_WIDTH))
```

```python
import functools

import numpy as np
import jax
import jax.numpy as jnp
from jax import lax
from jax.experimental import pallas as pl
from jax.experimental.pallas import tpu as pltpu

D_MODEL = 1024
DEPTH = 4
PAST_LEN = 16384
GM_WIDTH = D_MODEL
GM_GROUPS = 4
GM_GROUP_DIM = GM_WIDTH // GM_GROUPS
GM_CHUNK = 128
RET_HEADS = 4
RET_DK = D_MODEL // RET_HEADS
RET_DV = 2 * RET_DK
RET_QK = RET_HEADS * RET_DK
RET_V = RET_HEADS * RET_DV
ROPE_BASE = 10000.0
D_FF = 4 * D_MODEL
EPS = 1e-6

OFF_U = 0
OFF_V = GM_WIDTH
OFF_Q = 2 * GM_WIDTH
OFF_K = OFF_Q + RET_QK
OFF_VR = OFF_K + RET_QK
OFF_GR = OFF_VR + RET_V
OFF_GA = OFF_GR + RET_V
OFF_GB = OFF_GA + D_MODEL
IN_WIDTH = OFF_GB + D_MODEL

V7X_VMEM_BYTES = 64 * 1024 * 1024
V7X_LANES = 128
V7X_SUBLANES = 8
VMEM_REQUEST_FLOOR = V7X_VMEM_BYTES * 7 // 8

MIX_TILE = 512
RET_CHUNK = 256
FFN_TILE = 256
FF_SPLIT = 1024

BF16 = jnp.bfloat16
F32 = jnp.float32


def _dot(a, b):
    return jnp.dot(a, b, preferred_element_type=F32)


def _dot_nt(a, b):
    return lax.dot_general(a, b, (((1,), (1,)), ((), ())), preferred_element_type=F32)


def _dot_tn(a, b):
    return lax.dot_general(a, b, (((0,), (0,)), ((), ())), preferred_element_type=F32)


def _rms(x, g):
    return x * lax.rsqrt(jnp.mean(x * x, axis=-1, keepdims=True) + EPS) * g


def _layer_norm(x, g, b):
    mu = jnp.mean(x, axis=-1, keepdims=True)
    xc = x - mu
    return xc * lax.rsqrt(jnp.mean(xc * xc, axis=-1, keepdims=True) + EPS) * g + b


def _rotate(x, cos, sin):
    half = RET_DK // 2
    x1 = x[:, :half]
    x2 = x[:, half:]
    return jnp.concatenate([x1 * cos - x2 * sin, x1 * sin + x2 * cos], axis=-1)


def _head_gammas():
    return tuple(1.0 - 2.0 ** (-5.0 - h) for h in range(RET_HEADS))


def _full(shape):
    n = len(shape)
    return pl.BlockSpec(shape, lambda *_: (0,) * n)


def _window_bytes(shape, dtype):
    itemsize = jnp.dtype(dtype).itemsize
    dims = [d for d in shape if d is not None]
    dims[-1] = -(-dims[-1] // V7X_LANES) * V7X_LANES
    if len(dims) > 1:
        sub = V7X_SUBLANES * 4 // itemsize
        dims[-2] = -(-dims[-2] // sub) * sub
    return int(np.prod(dims)) * itemsize


def _vmem_limit(whole, streamed, temp_tiles):
    need = (sum(_window_bytes(s, d) for s, d in whole)
            + 2 * sum(_window_bytes(s, d) for s, d in streamed)
            + temp_tiles * _window_bytes((RET_CHUNK, D_MODEL), F32))
    assert need <= V7X_VMEM_BYTES, need
    return max(need, VMEM_REQUEST_FLOOR)


def _cast_specs(stacked, layer, steps, step_of):
    _, rows, cols = stacked.shape
    r = rows // steps
    assert r * steps == rows
    src = pl.BlockSpec((None, r, cols), lambda *idx: (layer, step_of(*idx), 0))
    dst = pl.BlockSpec((r, cols), lambda *idx: (step_of(*idx), 0))
    return src, dst


def _cast_rows(cast_in, cast_out):
    for src, dst in zip(cast_in, cast_out):
        dst[...] = src[...].astype(BF16)


def _mixer_kernel(*refs, chunk_decay, n_cast):
    (x_ref, cos_ref, sin_ref, ng_ref, win_ref, bg_ref, lng_ref, lnb_ref, ws_ref, bst_ref,
     intra_ref, qdec_ref, kdec_ref, wpg_ref, wpr_ref, wo_ref) = refs[:16]
    cast_in = refs[16:16 + n_cast]
    y_ref, s_ref = refs[-4 - n_cast:-2 - n_cast]
    cast_out = refs[-2 - n_cast:-2]
    a_scr, r_scr = refs[-2:]
    c = RET_CHUNK
    n_chunks = x_ref.shape[0] // c

    @pl.when(pl.program_id(1) == 0)
    def _():
        s_ref[...] = jnp.zeros(s_ref.shape, F32)

    row = lax.broadcasted_iota(jnp.int32, (GM_CHUNK, GM_CHUNK), 0)
    col = lax.broadcasted_iota(jnp.int32, (GM_CHUNK, GM_CHUNK), 1)

    def chunk_body(i):
        rows = pl.ds(i * c, c)
        x = x_ref[rows, :]
        xn = _rms(x, ng_ref[...]).astype(BF16)

        u = jax.nn.gelu(_dot(xn, win_ref[:, OFF_U:OFF_U + GM_WIDTH]))
        v = jax.nn.gelu(_dot(xn, win_ref[:, OFF_V:OFF_V + GM_WIDTH]))
        ga = jax.nn.sigmoid(_dot(xn, win_ref[:, OFF_GA:OFF_GA + D_MODEL]) + bg_ref[:, :D_MODEL])
        gb = jax.nn.sigmoid(_dot(xn, win_ref[:, OFF_GB:OFF_GB + D_MODEL]) + bg_ref[:, D_MODEL:])

        vb = _layer_norm(v, lng_ref[...], lnb_ref[...]).astype(BF16)
        for g in range(GM_GROUPS):
            w_tril = jnp.where(row >= col, ws_ref[g], 0.0).astype(BF16)
            b_col = bst_ref[:, g:g + 1]
            cs = slice(g * GM_GROUP_DIM, (g + 1) * GM_GROUP_DIM)
            for k in range(c // GM_CHUNK):
                rs = slice(k * GM_CHUNK, (k + 1) * GM_CHUNK)
                mixed = _dot(w_tril, vb[rs, cs]) + b_col
                a_scr[rs, cs] = (u[rs, cs] * mixed).astype(BF16)

        cos = cos_ref[rows, :]
        sin = sin_ref[rows, :]
        for h in range(RET_HEADS):
            q = _dot(xn, win_ref[:, OFF_Q + h * RET_DK:OFF_Q + (h + 1) * RET_DK])
            k = _dot(xn, win_ref[:, OFF_K + h * RET_DK:OFF_K + (h + 1) * RET_DK])
            vr = _dot(xn, win_ref[:, OFF_VR + h * RET_DV:OFF_VR + (h + 1) * RET_DV]).astype(BF16)
            gr = _dot(xn, win_ref[:, OFF_GR + h * RET_DV:OFF_GR + (h + 1) * RET_DV])
            qr = _rotate(q, cos, sin).astype(BF16)
            kr = _rotate(k, cos, sin) * (RET_DK ** -0.5)
            s_old = s_ref[h]
            scores = _dot_nt(qr, kr.astype(BF16)) * intra_ref[h]
            cross = _dot(qr, s_old.astype(BF16)) * qdec_ref[h]
            s_ref[h] = s_old * chunk_decay[h] + _dot_tn((kr * kdec_ref[h]).astype(BF16), vr)
            o = _dot(scores.astype(BF16), vr) + cross
            on = o * lax.rsqrt(jnp.mean(o * o, axis=-1, keepdims=True) + EPS)
            r_scr[:, h * RET_DV:(h + 1) * RET_DV] = (jax.nn.silu(gr) * on).astype(BF16)

        hm = ga * _dot(a_scr[...], wpg_ref[...]) + gb * _dot(r_scr[...], wpr_ref[...])
        y_ref[rows, :] = x + _dot(hm.astype(BF16), wo_ref[...])

        for src, dst in zip(cast_in, cast_out):
            r = src.shape[0] // n_chunks
            part = pl.ds(i * r, r)
            dst[part, :] = src[part, :].astype(BF16)

    for i in range(n_chunks):
        chunk_body(i)


def _retention_constants(c):
    log_g = np.log(np.array(_head_gammas(), np.float64))
    idx = np.arange(c, dtype=np.float64)
    diff = idx[:, None] - idx[None, :]
    intra = np.where(diff[None] >= 0, np.exp(diff[None] * log_g[:, None, None]), 0.0)
    qdec = np.exp((idx[None, :, None] + 1.0) * log_g[:, None, None])
    kdec = np.exp((c - 1.0 - idx[None, :, None]) * log_g[:, None, None])
    chunk_decay = tuple(float(v) for v in np.exp(c * log_g).astype(np.float32))
    return (jnp.asarray(intra, F32), jnp.asarray(qdec, F32), jnp.asarray(kdec, F32), chunk_decay)


def _rope_tables(pos0, length):
    inv_freq = 1.0 / (ROPE_BASE ** jnp.linspace(0.0, 1.0, RET_DK // 2, dtype=F32))
    pos = (jnp.arange(length, dtype=jnp.int32) + pos0).astype(F32)
    ang = pos[:, None] * inv_freq[None, :]
    return jnp.cos(ang), jnp.sin(ang)


def _mixer_call(layer, x, cos, sin, ng, win, bg, lng, lnb, ws, bst, intra, qdec, kdec, wpg, wpr,
                wo, chunk_decay, prev_states, cast_srcs):
    b, l, d = x.shape
    t = MIX_TILE
    per_row = l // t
    steps = b * per_row
    tok = pl.BlockSpec((None, t, d), lambda i, j: (i, j, 0))
    rope = pl.BlockSpec((t, RET_DK // 2), lambda i, j: (j, 0))
    in_specs = [tok, rope, rope, _full(ng.shape), _full(win.shape), _full(bg.shape),
                _full(lng.shape), _full(lnb.shape), _full(ws.shape), _full(bst.shape),
                _full(intra.shape), _full(qdec.shape), _full(kdec.shape),
                _full(wpg.shape), _full(wpr.shape), _full(wo.shape)]
    args = [x, cos, sin, ng, win, bg, lng, lnb, ws, bst, intra, qdec, kdec, wpg, wpr, wo]
    cast_out_specs = []
    cast_out_shapes = []
    for w in cast_srcs:
        src, dst = _cast_specs(w, layer, steps, lambda i, j: i * per_row + j)
        in_specs.append(src)
        args.append(w)
        cast_out_specs.append(dst)
        cast_out_shapes.append(jax.ShapeDtypeStruct(w.shape[1:], BF16))
    aliases = {}
    if prev_states is not None:
        in_specs.append(pl.BlockSpec(memory_space=pl.ANY))
        args.append(prev_states)
        aliases = {len(args) - 1: 1}
    state_spec = pl.BlockSpec((None, None, RET_HEADS, RET_DK, RET_DV),
                              lambda i, j: (layer, i, 0, 0, 0))
    scratch = [((RET_CHUNK, GM_WIDTH), BF16), ((RET_CHUNK, RET_V), BF16)]
    vmem = _vmem_limit(
        whole=[(a.shape, a.dtype) for a in args[3:16]] + scratch,
        streamed=[((t, d), F32)] * 2 + [((t, RET_DK // 2), F32)] * 2
        + [((RET_HEADS, RET_DK, RET_DV), F32)]
        + [(s.block_shape, dt) for s, dt in zip(in_specs[16:16 + len(cast_srcs)],
                                                 [F32] * len(cast_srcs))]
        + [(s.block_shape, BF16) for s in cast_out_specs],
        temp_tiles=10)
    outs = pl.pallas_call(
        functools.partial(_mixer_kernel, chunk_decay=chunk_decay, n_cast=len(cast_srcs)),
        grid=(b, per_row), in_specs=in_specs,
        out_specs=[tok, state_spec] + cast_out_specs,
        out_shape=[jax.ShapeDtypeStruct((b, l, d), F32),
                   jax.ShapeDtypeStruct((DEPTH, b, RET_HEADS, RET_DK, RET_DV), F32)]
        + cast_out_shapes,
        scratch_shapes=[pltpu.VMEM(s, dt) for s, dt in scratch],
        input_output_aliases=aliases,
        compiler_params=pltpu.CompilerParams(
            dimension_semantics=("arbitrary", "arbitrary"), vmem_limit_bytes=vmem),
        name="prompt_mixer",
    )(*args)
    return outs[0], outs[1], tuple(outs[2:])


def _ffn_body(x, hn, wup_ref, wdn_ref, anchor=None):
    acc = x
    for j in range(D_FF // FF_SPLIT):
        cols = slice(j * FF_SPLIT, (j + 1) * FF_SPLIT)
        up = _dot(hn, wup_ref[:, cols])
        if anchor is not None:
            up = up + anchor[:, cols]
        act = jnp.square(jnp.maximum(up, 0.0)).astype(BF16)
        acc = acc + _dot(act, wdn_ref[cols, :])
    return acc


def _zero_of(v):
    bits = pltpu.bitcast(v, jnp.int32)
    bits = lax.shift_right_logical(lax.shift_right_logical(bits, 16), 16)
    return bits.astype(F32)


def _ffn_state_kernel(*refs, final, gammas, seqs, n_cast):
    x_ref, g_ref, wup_ref, wdn_ref, gf_ref, s_ref, qk_ref, v_ref = refs[:8]
    cast_in = refs[8:8 + n_cast]
    y_ref, snew_ref, o_ref = refs[-3 - n_cast:len(refs) - n_cast]
    cast_out = refs[len(refs) - n_cast:]

    _cast_rows(cast_in, cast_out)

    zeros = []
    for j in range(seqs):
        for h in range(RET_HEADS):
            q_col = qk_ref[:, j * RET_HEADS + h:j * RET_HEADS + h + 1]
            k_col = qk_ref[:, (seqs + j) * RET_HEADS + h:(seqs + j) * RET_HEADS + h + 1]
            v_row = v_ref[j:j + 1, h * RET_DV:(h + 1) * RET_DV]
            snew_ref[j, h] = s_ref[j, h] * gammas[h] + k_col * v_row
            o = jnp.sum(q_col * snew_ref[j, h], axis=0, keepdims=True)
            o_ref[j:j + 1, h * RET_DV:(h + 1) * RET_DV] = o
            zeros.append(_zero_of(o))

    anchor = jnp.concatenate(zeros, axis=-1)
    assert anchor.shape == (1, D_FF)
    x = x_ref[...]
    y = _ffn_body(x, _rms(x, g_ref[...]).astype(BF16), wup_ref, wdn_ref, anchor)
    if final:
        y = _rms(y, gf_ref[...])
    y_ref[...] = y


def _ffn_state_call(layer, x2d, g, wup, wdn, gf, final, state_ret, q_s, k_s, v_s, prev_out,
                    cast_srcs):
    tokens, d = x2d.shape
    n = q_s.shape[0]
    t = FFN_TILE
    steps = tokens // t
    seqs = n // steps
    assert steps * t == tokens and seqs * steps == n

    q4 = q_s.reshape(steps, seqs, RET_HEADS, RET_DK)
    k4 = k_s.reshape(steps, seqs, RET_HEADS, RET_DK)
    qk = jnp.stack([q4, k4], axis=0).transpose(1, 4, 0, 2, 3).reshape(
        steps, RET_DK, 2 * seqs * RET_HEADS)
    v3 = v_s.reshape(steps, seqs, RET_V)

    tok = pl.BlockSpec((t, d), lambda i: (i, 0))
    s_spec = pl.BlockSpec((None, seqs, RET_HEADS, RET_DK, RET_DV), lambda i: (layer, i, 0, 0, 0))
    qk_spec = pl.BlockSpec((None, RET_DK, 2 * seqs * RET_HEADS), lambda i: (i, 0, 0))
    v_spec = pl.BlockSpec((None, seqs, RET_V), lambda i: (i, 0, 0))
    in_specs = [tok, _full(g.shape), _full(wup.shape), _full(wdn.shape), _full(gf.shape),
                s_spec, qk_spec, v_spec]
    args = [x2d, g, wup, wdn, gf, state_ret, qk, v3]
    cast_out_specs = []
    cast_out_shapes = []
    for w in cast_srcs:
        src, dst = _cast_specs(w, layer + 1, steps, lambda i: i)
        in_specs.append(src)
        args.append(w)
        cast_out_specs.append(dst)
        cast_out_shapes.append(jax.ShapeDtypeStruct(w.shape[1:], BF16))
    aliases = {}
    if prev_out is not None:
        in_specs.append(pl.BlockSpec(memory_space=pl.ANY))
        args.append(prev_out)
        aliases = {len(args) - 1: 1}
    vmem = _vmem_limit(
        whole=[(a.shape, a.dtype) for a in (g, wup, wdn, gf)],
        streamed=[(tok.block_shape, F32), (s_spec.block_shape, F32)] * 2
        + [(qk_spec.block_shape, F32)] + [(v_spec.block_shape, F32)] * 2
        + [(s.block_shape, F32) for s in in_specs[8:8 + len(cast_srcs)]]
        + [(s.block_shape, BF16) for s in cast_out_specs],
        temp_tiles=6)
    outs = pl.pallas_call(
        functools.partial(_ffn_state_kernel, final=final, gammas=_head_gammas(), seqs=seqs,
                          n_cast=len(cast_srcs)),
        grid=(steps,), in_specs=in_specs,
        out_specs=[tok, s_spec, v_spec] + cast_out_specs,
        out_shape=[jax.ShapeDtypeStruct((tokens, d), F32),
                   jax.ShapeDtypeStruct(state_ret.shape, F32),
                   jax.ShapeDtypeStruct((steps, seqs, RET_V), F32)] + cast_out_shapes,
        input_output_aliases=aliases,
        compiler_params=pltpu.CompilerParams(
            dimension_semantics=("arbitrary",), vmem_limit_bytes=vmem),
        name="prompt_ffn_sample_state",
    )(*args)
    return outs[0], outs[1], outs[2].reshape(n, RET_V), tuple(outs[3:])


def _sample_pre_kernel(x_ref, cos_ref, sin_ref, ng_ref, win_ref, bg_ref, lng_ref, lnb_ref,
                       wsd_ref, bsd_ref, v_ref, a_ref, q_ref, k_ref, vr_ref, sg_ref, gate_ref):
    xn = _rms(x_ref[...], ng_ref[...]).astype(BF16)
    u = jax.nn.gelu(_dot(xn, win_ref[:, OFF_U:OFF_U + GM_WIDTH]))
    v = jax.nn.gelu(_dot(xn, win_ref[:, OFF_V:OFF_V + GM_WIDTH]))
    v = _layer_norm(v, lng_ref[...], lnb_ref[...])
    v_ref[...] = v
    a_ref[...] = (u * (wsd_ref[...] * v + bsd_ref[...])).astype(BF16)
    cos = cos_ref[...]
    sin = sin_ref[...]
    for h in range(RET_HEADS):
        qs = slice(h * RET_DK, (h + 1) * RET_DK)
        q = _dot(xn, win_ref[:, OFF_Q + h * RET_DK:OFF_Q + (h + 1) * RET_DK])
        k = _dot(xn, win_ref[:, OFF_K + h * RET_DK:OFF_K + (h + 1) * RET_DK])
        q_ref[:, qs] = _rotate(q, cos, sin)
        k_ref[:, qs] = _rotate(k, cos, sin) * (RET_DK ** -0.5)
    vr_ref[...] = _dot(xn, win_ref[:, OFF_VR:OFF_VR + RET_V])
    sg_ref[...] = jax.nn.silu(_dot(xn, win_ref[:, OFF_GR:OFF_GR + RET_V]))
    gate_ref[...] = jax.nn.sigmoid(_dot(xn, win_ref[:, OFF_GA:OFF_GA + 2 * D_MODEL]) + bg_ref[...])


def _sample_pre_call(x, cos, sin, ng, win, bg, lng, lnb, wsd, bsd):
    n = x.shape[0]
    out_shape = [jax.ShapeDtypeStruct((n, GM_WIDTH), F32),
                 jax.ShapeDtypeStruct((n, GM_WIDTH), BF16),
                 jax.ShapeDtypeStruct((n, RET_QK), F32),
                 jax.ShapeDtypeStruct((n, RET_QK), F32),
                 jax.ShapeDtypeStruct((n, RET_V), F32),
                 jax.ShapeDtypeStruct((n, RET_V), F32),
                 jax.ShapeDtypeStruct((n, 2 * D_MODEL), F32)]
    args = [x, cos, sin, ng, win, bg, lng, lnb, wsd, bsd]
    vmem = _vmem_limit(whole=[(a.shape, a.dtype) for a in args + out_shape], streamed=[],
                       temp_tiles=8)
    return pl.pallas_call(
        _sample_pre_kernel, grid=(1,), in_specs=[_full(a.shape) for a in args],
        out_specs=[_full(s.shape) for s in out_shape], out_shape=out_shape,
        compiler_params=pltpu.CompilerParams(
            dimension_semantics=("arbitrary",), vmem_limit_bytes=vmem),
        name="sample_pre",
    )(*args)


def _sample_post_kernel(x_ref, o_ref, sg_ref, gate_ref, a_ref, wpg_ref, wpr_ref, wo_ref,
                        nfg_ref, wup_ref, wdn_ref, gf_ref, y_ref, *, final):
    rs = []
    for h in range(RET_HEADS):
        hs = slice(h * RET_DV, (h + 1) * RET_DV)
        o = o_ref[:, hs]
        on = o * lax.rsqrt(jnp.mean(o * o, axis=-1, keepdims=True) + EPS)
        rs.append((sg_ref[:, hs] * on).astype(BF16))
    r = jnp.concatenate(rs, axis=-1)
    hm = (gate_ref[:, :D_MODEL] * _dot(a_ref[...], wpg_ref[...])
          + gate_ref[:, D_MODEL:] * _dot(r, wpr_ref[...]))
    x = x_ref[...] + _dot(hm.astype(BF16), wo_ref[...])
    y = _ffn_body(x, _rms(x, nfg_ref[...]).astype(BF16), wup_ref, wdn_ref)
    if final:
        y = _rms(y, gf_ref[...])
    y_ref[...] = y


def _sample_post_call(x, o, sg, gate, a, wpg, wpr, wo, nfg, wup, wdn, gf, final):
    args = [x, o, sg, gate, a, wpg, wpr, wo, nfg, wup, wdn, gf]
    vmem = _vmem_limit(whole=[(v.shape, v.dtype) for v in args + [x]], streamed=[], temp_tiles=8)
    return pl.pallas_call(
        functools.partial(_sample_post_kernel, final=final),
        grid=(1,), in_specs=[_full(v.shape) for v in args], out_specs=_full(x.shape),
        out_shape=jax.ShapeDtypeStruct(x.shape, F32),
        compiler_params=pltpu.CompilerParams(
            dimension_semantics=("arbitrary",), vmem_limit_bytes=vmem),
        name="sample_post",
    )(*args)


def _sample_mid_kernel(*refs):
    n_post, n_pre = 12, 9
    y_ref = refs[n_post + n_pre]
    _sample_post_kernel(*refs[:n_post], y_ref, final=False)
    _sample_pre_kernel(y_ref, *refs[n_post:n_post + n_pre], *refs[n_post + n_pre + 1:])


def _sample_mid_call(post_args, pre_args):
    x = post_args[0]
    n = x.shape[0]
    args = list(post_args) + list(pre_args)
    out_shape = [jax.ShapeDtypeStruct(x.shape, F32),
                 jax.ShapeDtypeStruct((n, GM_WIDTH), F32),
                 jax.ShapeDtypeStruct((n, GM_WIDTH), BF16),
                 jax.ShapeDtypeStruct((n, RET_QK), F32),
                 jax.ShapeDtypeStruct((n, RET_QK), F32),
                 jax.ShapeDtypeStruct((n, RET_V), F32),
                 jax.ShapeDtypeStruct((n, RET_V), F32),
                 jax.ShapeDtypeStruct((n, 2 * D_MODEL), F32)]
    vmem = _vmem_limit(whole=[(a.shape, a.dtype) for a in args + out_shape], streamed=[],
                       temp_tiles=4)
    return pl.pallas_call(
        _sample_mid_kernel, grid=(1,), in_specs=[_full(a.shape) for a in args],
        out_specs=[_full(s.shape) for s in out_shape], out_shape=out_shape,
        compiler_params=pltpu.CompilerParams(
            dimension_semantics=("arbitrary",), vmem_limit_bytes=vmem),
        name="sample_mid",
    )(*args)


def kernel(x_prompt, x_sample, state_ret, norm_mix_g, w_in, b_gate, gm_ln_g, gm_ln_b, gm_w_s, gm_b_s,
           w_proj_gm, w_proj_ret, w_out, norm_ffn_g, w_up, w_down, norm_final_g):
    b, l, d = x_prompt.shape
    n = x_sample.shape[0]
    assert l % MIX_TILE == 0 and MIX_TILE % RET_CHUNK == 0 and x_sample.shape[1] == 1

    intra, qdec, kdec, chunk_decay = _retention_constants(RET_CHUNK)
    cos_p, sin_p = _rope_tables(0, l)
    cos_s, sin_s = _rope_tables(PAST_LEN, 1)
    row = lambda a: a.reshape(1, -1)
    gf = row(norm_final_g)

    mixer_f32 = (w_in, w_proj_gm, w_proj_ret, w_out)
    ffn_f32 = (w_up, w_down)
    win, wpg, wpr, wo = (w[0].astype(BF16) for w in mixer_f32)

    xp = x_prompt
    xs = x_sample.reshape(n, d)
    v_rows = []
    prompt_states = None
    sample_states = None
    def small(layer):
        return (row(norm_mix_g[layer]), row(b_gate[layer]), row(gm_ln_g[layer]),
                row(gm_ln_b[layer]),
                row(jnp.repeat(gm_w_s[layer, :, 0, 0], GM_GROUP_DIM)),
                row(jnp.repeat(gm_b_s[layer, :, 0], GM_GROUP_DIM)))

    ng, bg, lng, lnb, wsd, bsd = small(0)
    pre = _sample_pre_call(xs, cos_s, sin_s, ng, win, bg, lng, lnb, wsd, bsd)
    for layer in range(DEPTH):
        ng, bg, lng, lnb, wsd, bsd = small(layer)
        nfg = row(norm_ffn_g[layer])
        final = layer == DEPTH - 1
        v_s, a_s, q_s, k_s, vr_s, sg_s, gate_s = pre
        v_rows.append(v_s)

        xp, prompt_states, (wup, wdn) = _mixer_call(
            layer, xp, cos_p, sin_p, ng, win, bg, lng, lnb, gm_w_s[layer], gm_b_s[layer].T,
            intra, qdec, kdec, wpg, wpr, wo, chunk_decay, prompt_states, ffn_f32)
        xp, sample_states, o_s, next_mixer = _ffn_state_call(
            layer, xp.reshape(b * l, d), nfg, wup, wdn, gf, final, state_ret, q_s, k_s, vr_s,
            sample_states, () if final else mixer_f32)
        xp = xp.reshape(b, l, d)
        post_args = [xs, o_s, sg_s, gate_s, a_s, wpg, wpr, wo, nfg, wup, wdn, gf]
        if final:
            xs = _sample_post_call(*post_args, final)
        else:
            ng1, bg1, lng1, lnb1, wsd1, bsd1 = small(layer + 1)
            xs, *pre = _sample_mid_call(
                post_args, [cos_s, sin_s, ng1, next_mixer[0], bg1, lng1, lnb1, wsd1, bsd1])
            win, wpg, wpr, wo = next_mixer

    return (xp, xs.reshape(n, 1, d), prompt_states, sample_states,
            jnp.stack(v_rows).reshape(DEPTH, n, 1, GM_WIDTH))
```

```python
import functools

import numpy as np
import jax
import jax.numpy as jnp
from jax import lax
from jax.experimental import pallas as pl
from jax.experimental.pallas import tpu as pltpu

D_MODEL = 1024
DEPTH = 4
PAST_LEN = 16384
GM_WIDTH = D_MODEL
GM_GROUPS = 4
GM_GROUP_DIM = GM_WIDTH // GM_GROUPS
GM_CHUNK = 128
RET_HEADS = 4
RET_DK = D_MODEL // RET_HEADS
RET_DV = 2 * RET_DK
RET_QK = RET_HEADS * RET_DK
RET_V = RET_HEADS * RET_DV
ROPE_BASE = 10000.0
D_FF = 4 * D_MODEL
EPS = 1e-6

OFF_U = 0
OFF_V = GM_WIDTH
OFF_Q = 2 * GM_WIDTH
OFF_K = OFF_Q + RET_QK
OFF_VR = OFF_K + RET_QK
OFF_GR = OFF_VR + RET_V
OFF_GA = OFF_GR + RET_V
OFF_GB = OFF_GA + D_MODEL
IN_WIDTH = OFF_GB + D_MODEL

V7X_VMEM_BYTES = 64 * 1024 * 1024
V7X_LANES = 128
V7X_SUBLANES = 8
VMEM_REQUEST_FLOOR = V7X_VMEM_BYTES * 7 // 8

MIX_TILE = 512
RET_CHUNK = 256
FFN_TILE = 256
FF_SPLIT = 1024

BF16 = jnp.bfloat16
F32 = jnp.float32


def _dot(a, b):
    return jnp.dot(a, b, preferred_element_type=F32)


def _dot_nt(a, b):
    return lax.dot_general(a, b, (((1,), (1,)), ((), ())), preferred_element_type=F32)


def _dot_tn(a, b):
    return lax.dot_general(a, b, (((0,), (0,)), ((), ())), preferred_element_type=F32)


def _rms(x, g):
    return x * lax.rsqrt(jnp.mean(x * x, axis=-1, keepdims=True) + EPS) * g


def _layer_norm(x, g, b):
    mu = jnp.mean(x, axis=-1, keepdims=True)
    xc = x - mu
    return xc * lax.rsqrt(jnp.mean(xc * xc, axis=-1, keepdims=True) + EPS) * g + b


def _rotate(x, cos, sin):
    half = RET_DK // 2
    x1 = x[:, :half]
    x2 = x[:, half:]
    return jnp.concatenate([x1 * cos - x2 * sin, x1 * sin + x2 * cos], axis=-1)


def _head_gammas():
    return tuple(1.0 - 2.0 ** (-5.0 - h) for h in range(RET_HEADS))


def _full(shape):
    n = len(shape)
    return pl.BlockSpec(shape, lambda *_: (0,) * n)


def _window_bytes(shape, dtype):
    itemsize = jnp.dtype(dtype).itemsize
    dims = [d for d in shape if d is not None]
    dims[-1] = -(-dims[-1] // V7X_LANES) * V7X_LANES
    if len(dims) > 1:
        sub = V7X_SUBLANES * 4 // itemsize
        dims[-2] = -(-dims[-2] // sub) * sub
    return int(np.prod(dims)) * itemsize


def _vmem_limit(whole, streamed, temp_tiles):
    need = (sum(_window_bytes(s, d) for s, d in whole)
            + 2 * sum(_window_bytes(s, d) for s, d in streamed)
            + temp_tiles * _window_bytes((RET_CHUNK, D_MODEL), F32))
    assert need <= V7X_VMEM_BYTES, need
    return max(need, VMEM_REQUEST_FLOOR)


def _cast_specs(stacked, layer, steps, step_of):
    _, rows, cols = stacked.shape
    r = rows // steps
    assert r * steps == rows
    src = pl.BlockSpec((None, r, cols), lambda *idx: (layer, step_of(*idx), 0))
    dst = pl.BlockSpec((r, cols), lambda *idx: (step_of(*idx), 0))
    return src, dst


def _cast_rows(cast_in, cast_out):
    for src, dst in zip(cast_in, cast_out):
        dst[...] = src[...].astype(BF16)


def _mixer_kernel(*refs, chunk_decay, n_cast):
    (x_ref, cos_ref, sin_ref, ng_ref, win_ref, bg_ref, lng_ref, lnb_ref, ws_ref, bst_ref,
     intra_ref, qdec_ref, kdec_ref, wpg_ref, wpr_ref, wo_ref) = refs[:16]
    cast_in = refs[16:16 + n_cast]
    y_ref, s_ref = refs[-4 - n_cast:-2 - n_cast]
    cast_out = refs[-2 - n_cast:-2]
    a_scr, r_scr = refs[-2:]
    c = RET_CHUNK
    n_chunks = x_ref.shape[0] // c

    @pl.when(pl.program_id(1) == 0)
    def _():
        s_ref[...] = jnp.zeros(s_ref.shape, F32)

    row = lax.broadcasted_iota(jnp.int32, (GM_CHUNK, GM_CHUNK), 0)
    col = lax.broadcasted_iota(jnp.int32, (GM_CHUNK, GM_CHUNK), 1)

    def chunk_body(i):
        rows = pl.ds(i * c, c)
        x = x_ref[rows, :]
        xn = _rms(x, ng_ref[...]).astype(BF16)

        u = jax.nn.gelu(_dot(xn, win_ref[:, OFF_U:OFF_U + GM_WIDTH]))
        v = jax.nn.gelu(_dot(xn, win_ref[:, OFF_V:OFF_V + GM_WIDTH]))
        ga = jax.nn.sigmoid(_dot(xn, win_ref[:, OFF_GA:OFF_GA + D_MODEL]) + bg_ref[:, :D_MODEL])
        gb = jax.nn.sigmoid(_dot(xn, win_ref[:, OFF_GB:OFF_GB + D_MODEL]) + bg_ref[:, D_MODEL:])

        vb = _layer_norm(v, lng_ref[...], lnb_ref[...]).astype(BF16)
        for g in range(GM_GROUPS):
            w_tril = jnp.where(row >= col, ws_ref[g], 0.0).astype(BF16)
            b_col = bst_ref[:, g:g + 1]
            cs = slice(g * GM_GROUP_DIM, (g + 1) * GM_GROUP_DIM)
            for k in range(c // GM_CHUNK):
                rs = slice(k * GM_CHUNK, (k + 1) * GM_CHUNK)
                mixed = _dot(w_tril, vb[rs, cs]) + b_col
                a_scr[rs, cs] = (u[rs, cs] * mixed).astype(BF16)

        cos = cos_ref[rows, :]
        sin = sin_ref[rows, :]
        for h in range(RET_HEADS):
            q = _dot(xn, win_ref[:, OFF_Q + h * RET_DK:OFF_Q + (h + 1) * RET_DK])
            k = _dot(xn, win_ref[:, OFF_K + h * RET_DK:OFF_K + (h + 1) * RET_DK])
            vr = _dot(xn, win_ref[:, OFF_VR + h * RET_DV:OFF_VR + (h + 1) * RET_DV]).astype(BF16)
            gr = _dot(xn, win_ref[:, OFF_GR + h * RET_DV:OFF_GR + (h + 1) * RET_DV])
            qr = _rotate(q, cos, sin).astype(BF16)
            kr = _rotate(k, cos, sin) * (RET_DK ** -0.5)
            s_old = s_ref[h]
            scores = _dot_nt(qr, kr.astype(BF16)) * intra_ref[h]
            cross = _dot(qr, s_old.astype(BF16)) * qdec_ref[h]
            s_ref[h] = s_old * chunk_decay[h] + _dot_tn((kr * kdec_ref[h]).astype(BF16), vr)
            o = _dot(scores.astype(BF16), vr) + cross
            on = o * lax.rsqrt(jnp.mean(o * o, axis=-1, keepdims=True) + EPS)
            r_scr[:, h * RET_DV:(h + 1) * RET_DV] = (jax.nn.silu(gr) * on).astype(BF16)

        hm = ga * _dot(a_scr[...], wpg_ref[...]) + gb * _dot(r_scr[...], wpr_ref[...])
        y_ref[rows, :] = x + _dot(hm.astype(BF16), wo_ref[...])

        for src, dst in zip(cast_in, cast_out):
            r = src.shape[0] // n_chunks
            part = pl.ds(i * r, r)
            dst[part, :] = src[part, :].astype(BF16)

    for i in range(n_chunks):
        chunk_body(i)


def _retention_constants(c):
    log_g = np.log(np.array(_head_gammas(), np.float64))
    idx = np.arange(c, dtype=np.float64)
    diff = idx[:, None] - idx[None, :]
    intra = np.where(diff[None] >= 0, np.exp(diff[None] * log_g[:, None, None]), 0.0)
    qdec = np.exp((idx[None, :, None] + 1.0) * log_g[:, None, None])
    kdec = np.exp((c - 1.0 - idx[None, :, None]) * log_g[:, None, None])
    chunk_decay = tuple(float(v) for v in np.exp(c * log_g).astype(np.float32))
    return (jnp.asarray(intra, F32), jnp.asarray(qdec, F32), jnp.asarray(kdec, F32), chunk_decay)


def _rope_tables(pos0, length):
    inv_freq = 1.0 / (ROPE_BASE ** jnp.linspace(0.0, 1.0, RET_DK // 2, dtype=F32))
    pos = (jnp.arange(length, dtype=jnp.int32) + pos0).astype(F32)
    ang = pos[:, None] * inv_freq[None, :]
    return jnp.cos(ang), jnp.sin(ang)


def _mixer_call(layer, x, cos, sin, ng, win, bg, lng, lnb, ws, bst, intra, qdec, kdec, wpg, wpr,
                wo, chunk_decay, prev_states, cast_srcs):
    b, l, d = x.shape
    t = MIX_TILE
    per_row = l // t
    steps = b * per_row
    tok = pl.BlockSpec((None, t, d), lambda i, j: (i, j, 0))
    rope = pl.BlockSpec((t, RET_DK // 2), lambda i, j: (j, 0))
    in_specs = [tok, rope, rope, _full(ng.shape), _full(win.shape), _full(bg.shape),
                _full(lng.shape), _full(lnb.shape), _full(ws.shape), _full(bst.shape),
                _full(intra.shape), _full(qdec.shape), _full(kdec.shape),
                _full(wpg.shape), _full(wpr.shape), _full(wo.shape)]
    args = [x, cos, sin, ng, win, bg, lng, lnb, ws, bst, intra, qdec, kdec, wpg, wpr, wo]
    cast_out_specs = []
    cast_out_shapes = []
    for w in cast_srcs:
        src, dst = _cast_specs(w, layer, steps, lambda i, j: i * per_row + j)
        in_specs.append(src)
        args.append(w)
        cast_out_specs.append(dst)
        cast_out_shapes.append(jax.ShapeDtypeStruct(w.shape[1:], BF16))
    aliases = {}
    if prev_states is not None:
        in_specs.append(pl.BlockSpec(memory_space=pl.ANY))
        args.append(prev_states)
        aliases = {len(args) - 1: 1}
    state_spec = pl.BlockSpec((None, None, RET_HEADS, RET_DK, RET_DV),
                              lambda i, j: (layer, i, 0, 0, 0))
    scratch = [((RET_CHUNK, GM_WIDTH), BF16), ((RET_CHUNK, RET_V), BF16)]
    vmem = _vmem_limit(
        whole=[(a.shape, a.dtype) for a in args[3:16]] + scratch,
        streamed=[((t, d), F32)] * 2 + [((t, RET_DK // 2), F32)] * 2
        + [((RET_HEADS, RET_DK, RET_DV), F32)]
        + [(s.block_shape, dt) for s, dt in zip(in_specs[16:16 + len(cast_srcs)],
                                                 [F32] * len(cast_srcs))]
        + [(s.block_shape, BF16) for s in cast_out_specs],
        temp_tiles=10)
    outs = pl.pallas_call(
        functools.partial(_mixer_kernel, chunk_decay=chunk_decay, n_cast=len(cast_srcs)),
        grid=(b, per_row), in_specs=in_specs,
        out_specs=[tok, state_spec] + cast_out_specs,
        out_shape=[jax.ShapeDtypeStruct((b, l, d), F32),
                   jax.ShapeDtypeStruct((DEPTH, b, RET_HEADS, RET_DK, RET_DV), F32)]
        + cast_out_shapes,
        scratch_shapes=[pltpu.VMEM(s, dt) for s, dt in scratch],
        input_output_aliases=aliases,
        compiler_params=pltpu.CompilerParams(
            dimension_semantics=("parallel", "arbitrary"), vmem_limit_bytes=vmem),
        name="prompt_mixer",
    )(*args)
    return outs[0], outs[1], tuple(outs[2:])


def _ffn_body(x, hn, wup_ref, wdn_ref, anchor=None):
    acc = x
    for j in range(D_FF // FF_SPLIT):
        cols = slice(j * FF_SPLIT, (j + 1) * FF_SPLIT)
        up = _dot(hn, wup_ref[:, cols])
        if anchor is not None:
            up = up + anchor[:, cols]
        act = jnp.square(jnp.maximum(up, 0.0)).astype(BF16)
        acc = acc + _dot(act, wdn_ref[cols, :])
    return acc


def _zero_of(v):
    bits = pltpu.bitcast(v, jnp.int32)
    bits = lax.shift_right_logical(lax.shift_right_logical(bits, 16), 16)
    return bits.astype(F32)


def _ffn_state_kernel(*refs, final, gammas, seqs, n_cast):
    x_ref, g_ref, wup_ref, wdn_ref, gf_ref, s_ref, qk_ref, v_ref = refs[:8]
    cast_in = refs[8:8 + n_cast]
    y_ref, snew_ref, o_ref = refs[-3 - n_cast:len(refs) - n_cast]
    cast_out = refs[len(refs) - n_cast:]

    _cast_rows(cast_in, cast_out)

    zeros = []
    for j in range(seqs):
        for h in range(RET_HEADS):
            q_col = qk_ref[:, j * RET_HEADS + h:j * RET_HEADS + h + 1]
            k_col = qk_ref[:, (seqs + j) * RET_HEADS + h:(seqs + j) * RET_HEADS + h + 1]
            v_row = v_ref[j:j + 1, h * RET_DV:(h + 1) * RET_DV]
            snew_ref[j, h] = s_ref[j, h] * gammas[h] + k_col * v_row
            o = jnp.sum(q_col * snew_ref[j, h], axis=0, keepdims=True)
            o_ref[j:j + 1, h * RET_DV:(h + 1) * RET_DV] = o
            zeros.append(_zero_of(o))

    anchor = jnp.concatenate(zeros, axis=-1)
    assert anchor.shape == (1, D_FF)
    x = x_ref[...]
    y = _ffn_body(x, _rms(x, g_ref[...]).astype(BF16), wup_ref, wdn_ref, anchor)
    if final:
        y = _rms(y, gf_ref[...])
    y_ref[...] = y


def _ffn_state_call(layer, x2d, g, wup, wdn, gf, final, state_ret, q_s, k_s, v_s, prev_out,
                    cast_srcs):
    tokens, d = x2d.shape
    n = q_s.shape[0]
    t = FFN_TILE
    steps = tokens // t
    seqs = n // steps
    assert steps * t == tokens and seqs * steps == n

    q4 = q_s.reshape(steps, seqs, RET_HEADS, RET_DK)
    k4 = k_s.reshape(steps, seqs, RET_HEADS, RET_DK)
    qk = jnp.stack([q4, k4], axis=0).transpose(1, 4, 0, 2, 3).reshape(
        steps, RET_DK, 2 * seqs * RET_HEADS)
    v3 = v_s.reshape(steps, seqs, RET_V)

    tok = pl.BlockSpec((t, d), lambda i: (i, 0))
    s_spec = pl.BlockSpec((None, seqs, RET_HEADS, RET_DK, RET_DV), lambda i: (layer, i, 0, 0, 0))
    qk_spec = pl.BlockSpec((None, RET_DK, 2 * seqs * RET_HEADS), lambda i: (i, 0, 0))
    v_spec = pl.BlockSpec((None, seqs, RET_V), lambda i: (i, 0, 0))
    in_specs = [tok, _full(g.shape), _full(wup.shape), _full(wdn.shape), _full(gf.shape),
                s_spec, qk_spec, v_spec]
    args = [x2d, g, wup, wdn, gf, state_ret, qk, v3]
    cast_out_specs = []
    cast_out_shapes = []
    for w in cast_srcs:
        src, dst = _cast_specs(w, layer + 1, steps, lambda i: i)
        in_specs.append(src)
        args.append(w)
        cast_out_specs.append(dst)
        cast_out_shapes.append(jax.ShapeDtypeStruct(w.shape[1:], BF16))
    aliases = {}
    if prev_out is not None:
        in_specs.append(pl.BlockSpec(memory_space=pl.ANY))
        args.append(prev_out)
        aliases = {len(args) - 1: 1}
    vmem = _vmem_limit(
        whole=[(a.shape, a.dtype) for a in (g, wup, wdn, gf)],
        streamed=[(tok.block_shape, F32), (s_spec.block_shape, F32)] * 2
        + [(qk_spec.block_shape, F32)] + [(v_spec.block_shape, F32)] * 2
        + [(s.block_shape, F32) for s in in_specs[8:8 + len(cast_srcs)]]
        + [(s.block_shape, BF16) for s in cast_out_specs],
        temp_tiles=6)
    outs = pl.pallas_call(
        functools.partial(_ffn_state_kernel, final=final, gammas=_head_gammas(), seqs=seqs,
                          n_cast=len(cast_srcs)),
        grid=(steps,), in_specs=in_specs,
        out_specs=[tok, s_spec, v_spec] + cast_out_specs,
        out_shape=[jax.ShapeDtypeStruct((tokens, d), F32),
                   jax.ShapeDtypeStruct(state_ret.shape, F32),
                   jax.ShapeDtypeStruct((steps, seqs, RET_V), F32)] + cast_out_shapes,
        input_output_aliases=aliases,
        compiler_params=pltpu.CompilerParams(
            dimension_semantics=("parallel",), vmem_limit_bytes=vmem),
        name="prompt_ffn_sample_state",
    )(*args)
    return outs[0], outs[1], outs[2].reshape(n, RET_V), tuple(outs[3:])


def _sample_pre_kernel(x_ref, cos_ref, sin_ref, ng_ref, win_ref, bg_ref, lng_ref, lnb_ref,
                       wsd_ref, bsd_ref, v_ref, a_ref, q_ref, k_ref, vr_ref, sg_ref, gate_ref):
    xn = _rms(x_ref[...], ng_ref[...]).astype(BF16)
    u = jax.nn.gelu(_dot(xn, win_ref[:, OFF_U:OFF_U + GM_WIDTH]))
    v = jax.nn.gelu(_dot(xn, win_ref[:, OFF_V:OFF_V + GM_WIDTH]))
    v = _layer_norm(v, lng_ref[...], lnb_ref[...])
    v_ref[...] = v
    a_ref[...] = (u * (wsd_ref[...] * v + bsd_ref[...])).astype(BF16)
    cos = cos_ref[...]
    sin = sin_ref[...]
    for h in range(RET_HEADS):
        qs = slice(h * RET_DK, (h + 1) * RET_DK)
        q = _dot(xn, win_ref[:, OFF_Q + h * RET_DK:OFF_Q + (h + 1) * RET_DK])
        k = _dot(xn, win_ref[:, OFF_K + h * RET_DK:OFF_K + (h + 1) * RET_DK])
        q_ref[:, qs] = _rotate(q, cos, sin)
        k_ref[:, qs] = _rotate(k, cos, sin) * (RET_DK ** -0.5)
    vr_ref[...] = _dot(xn, win_ref[:, OFF_VR:OFF_VR + RET_V])
    sg_ref[...] = jax.nn.silu(_dot(xn, win_ref[:, OFF_GR:OFF_GR + RET_V]))
    gate_ref[...] = jax.nn.sigmoid(_dot(xn, win_ref[:, OFF_GA:OFF_GA + 2 * D_MODEL]) + bg_ref[...])


def _sample_pre_call(x, cos, sin, ng, win, bg, lng, lnb, wsd, bsd):
    n = x.shape[0]
    out_shape = [jax.ShapeDtypeStruct((n, GM_WIDTH), F32),
                 jax.ShapeDtypeStruct((n, GM_WIDTH), BF16),
                 jax.ShapeDtypeStruct((n, RET_QK), F32),
                 jax.ShapeDtypeStruct((n, RET_QK), F32),
                 jax.ShapeDtypeStruct((n, RET_V), F32),
                 jax.ShapeDtypeStruct((n, RET_V), F32),
                 jax.ShapeDtypeStruct((n, 2 * D_MODEL), F32)]
    args = [x, cos, sin, ng, win, bg, lng, lnb, wsd, bsd]
    vmem = _vmem_limit(whole=[(a.shape, a.dtype) for a in args + out_shape], streamed=[],
                       temp_tiles=8)
    return pl.pallas_call(
        _sample_pre_kernel, grid=(1,), in_specs=[_full(a.shape) for a in args],
        out_specs=[_full(s.shape) for s in out_shape], out_shape=out_shape,
        compiler_params=pltpu.CompilerParams(
            dimension_semantics=("arbitrary",), vmem_limit_bytes=vmem),
        name="sample_pre",
    )(*args)


def _sample_post_kernel(x_ref, o_ref, sg_ref, gate_ref, a_ref, wpg_ref, wpr_ref, wo_ref,
                        nfg_ref, wup_ref, wdn_ref, gf_ref, y_ref, *, final):
    rs = []
    for h in range(RET_HEADS):
        hs = slice(h * RET_DV, (h + 1) * RET_DV)
        o = o_ref[:, hs]
        on = o * lax.rsqrt(jnp.mean(o * o, axis=-1, keepdims=True) + EPS)
        rs.append((sg_ref[:, hs] * on).astype(BF16))
    r = jnp.concatenate(rs, axis=-1)
    hm = (gate_ref[:, :D_MODEL] * _dot(a_ref[...], wpg_ref[...])
          + gate_ref[:, D_MODEL:] * _dot(r, wpr_ref[...]))
    x = x_ref[...] + _dot(hm.astype(BF16), wo_ref[...])
    y = _ffn_body(x, _rms(x, nfg_ref[...]).astype(BF16), wup_ref, wdn_ref)
    if final:
        y = _rms(y, gf_ref[...])
    y_ref[...] = y


def _sample_post_call(x, o, sg, gate, a, wpg, wpr, wo, nfg, wup, wdn, gf, final):
    args = [x, o, sg, gate, a, wpg, wpr, wo, nfg, wup, wdn, gf]
    vmem = _vmem_limit(whole=[(v.shape, v.dtype) for v in args + [x]], streamed=[], temp_tiles=8)
    return pl.pallas_call(
        functools.partial(_sample_post_kernel, final=final),
        grid=(1,), in_specs=[_full(v.shape) for v in args], out_specs=_full(x.shape),
        out_shape=jax.ShapeDtypeStruct(x.shape, F32),
        compiler_params=pltpu.CompilerParams(
            dimension_semantics=("arbitrary",), vmem_limit_bytes=vmem),
        name="sample_post",
    )(*args)


def _sample_mid_kernel(*refs):
    n_post, n_pre = 12, 9
    y_ref = refs[n_post + n_pre]
    _sample_post_kernel(*refs[:n_post], y_ref, final=False)
    _sample_pre_kernel(y_ref, *refs[n_post:n_post + n_pre], *refs[n_post + n_pre + 1:])


def _sample_mid_call(post_args, pre_args):
    x = post_args[0]
    n = x.shape[0]
    args = list(post_args) + list(pre_args)
    out_shape = [jax.ShapeDtypeStruct(x.shape, F32),
                 jax.ShapeDtypeStruct((n, GM_WIDTH), F32),
                 jax.ShapeDtypeStruct((n, GM_WIDTH), BF16),
                 jax.ShapeDtypeStruct((n, RET_QK), F32),
                 jax.ShapeDtypeStruct((n, RET_QK), F32),
                 jax.ShapeDtypeStruct((n, RET_V), F32),
                 jax.ShapeDtypeStruct((n, RET_V), F32),
                 jax.ShapeDtypeStruct((n, 2 * D_MODEL), F32)]
    vmem = _vmem_limit(whole=[(a.shape, a.dtype) for a in args + out_shape], streamed=[],
                       temp_tiles=4)
    return pl.pallas_call(
        _sample_mid_kernel, grid=(1,), in_specs=[_full(a.shape) for a in args],
        out_specs=[_full(s.shape) for s in out_shape], out_shape=out_shape,
        compiler_params=pltpu.CompilerParams(
            dimension_semantics=("arbitrary",), vmem_limit_bytes=vmem),
        name="sample_mid",
    )(*args)


def kernel(x_prompt, x_sample, state_ret, norm_mix_g, w_in, b_gate, gm_ln_g, gm_ln_b, gm_w_s, gm_b_s,
           w_proj_gm, w_proj_ret, w_out, norm_ffn_g, w_up, w_down, norm_final_g):
    b, l, d = x_prompt.shape
    n = x_sample.shape[0]
    assert l % MIX_TILE == 0 and MIX_TILE % RET_CHUNK == 0 and x_sample.shape[1] == 1

    intra, qdec, kdec, chunk_decay = _retention_constants(RET_CHUNK)
    cos_p, sin_p = _rope_tables(0, l)
    cos_s, sin_s = _rope_tables(PAST_LEN, 1)
    row = lambda a: a.reshape(1, -1)
    gf = row(norm_final_g)

    mixer_f32 = (w_in, w_proj_gm, w_proj_ret, w_out)
    ffn_f32 = (w_up, w_down)
    win, wpg, wpr, wo = (w[0].astype(BF16) for w in mixer_f32)

    xp = x_prompt
    xs = x_sample.reshape(n, d)
    v_rows = []
    prompt_states = None
    sample_states = None
    def small(layer):
        return (row(norm_mix_g[layer]), row(b_gate[layer]), row(gm_ln_g[layer]),
                row(gm_ln_b[layer]),
                row(jnp.repeat(gm_w_s[layer, :, 0, 0], GM_GROUP_DIM)),
                row(jnp.repeat(gm_b_s[layer, :, 0], GM_GROUP_DIM)))

    ng, bg, lng, lnb, wsd, bsd = small(0)
    pre = _sample_pre_call(xs, cos_s, sin_s, ng, win, bg, lng, lnb, wsd, bsd)
    for layer in range(DEPTH):
        ng, bg, lng, lnb, wsd, bsd = small(layer)
        nfg = row(norm_ffn_g[layer])
        final = layer == DEPTH - 1
        v_s, a_s, q_s, k_s, vr_s, sg_s, gate_s = pre
        v_rows.append(v_s)

        xp, prompt_states, (wup, wdn) = _mixer_call(
            layer, xp, cos_p, sin_p, ng, win, bg, lng, lnb, gm_w_s[layer], gm_b_s[layer].T,
            intra, qdec, kdec, wpg, wpr, wo, chunk_decay, prompt_states, ffn_f32)
        xp, sample_states, o_s, next_mixer = _ffn_state_call(
            layer, xp.reshape(b * l, d), nfg, wup, wdn, gf, final, state_ret, q_s, k_s, vr_s,
            sample_states, () if final else mixer_f32)
        xp = xp.reshape(b, l, d)
        post_args = [xs, o_s, sg_s, gate_s, a_s, wpg, wpr, wo, nfg, wup, wdn, gf]
        if final:
            xs = _sample_post_call(*post_args, final)
        else:
            ng1, bg1, lng1, lnb1, wsd1, bsd1 = small(layer + 1)
            xs, *pre = _sample_mid_call(
                post_args, [cos_s, sin_s, ng1, next_mixer[0], bg1, lng1, lnb1, wsd1, bsd1])
            win, wpg, wpr, wo = next_mixer

    return (xp, xs.reshape(n, 1, d), prompt_states, sample_states,
            jnp.stack(v_rows).reshape(DEPTH, n, 1, GM_WIDTH))
```
